```python
import math
import jax, jax.numpy as jnp
from jax import lax
import numpy as np


D_MODEL = 1024
BATCH = 4
SEQ = 8192
DEPTH = 1

CHUNK = 64
Q_BLOCK = 128
RMS_EPS = 1e-6
ROPE_THETA = 10000.0
MAX_STREAM_OFFSET_CHUNKS = 512

DA_HEADS = 8
DA_HEAD_DIM = 64
DA_QK_WIDTH = DA_HEADS * 2 * DA_HEAD_DIM
DA_V_DIM = 2 * DA_HEAD_DIM
DA_V_WIDTH = DA_HEADS * DA_V_DIM

MLA_HEADS = 8
MLA_Q_RANK = 384
MLA_KV_RANK = 256
MLA_NOPE_DIM = 128
MLA_ROPE_DIM = 64
MLA_V_DIM = 128
MLA_QK_DIM = MLA_NOPE_DIM + MLA_ROPE_DIM
MLA_V_WIDTH = MLA_HEADS * MLA_V_DIM

IN_SPLITS = (DA_QK_WIDTH, DA_QK_WIDTH, DA_V_WIDTH, MLA_Q_RANK, MLA_KV_RANK, MLA_ROPE_DIM,
             D_MODEL, D_MODEL)
IN_COLS = sum(IN_SPLITS)

D_FF = ((8 * D_MODEL // 3 + 255) // 256) * 256

kernel_name = "hybrid_diffattn_mla_gated_swiglu"


def rms_norm(x, w):
    xf = x.astype(jnp.float32)
    y = xf * lax.rsqrt(jnp.mean(xf * xf, axis=-1, keepdims=True) + RMS_EPS)
    return (y * w.astype(jnp.float32)).astype(x.dtype)


def rope(x, positions):
    d = x.shape[-1]
    half = d // 2
    inv_freq = 1.0 / (ROPE_THETA ** (jnp.arange(half, dtype=jnp.float32) * (2.0 / d)))
    ang = positions.astype(jnp.float32)[:, None, :, None] * inv_freq
    cos, sin = jnp.cos(ang), jnp.sin(ang)
    x1 = x[..., :half].astype(jnp.float32)
    x2 = x[..., half:].astype(jnp.float32)
    out = jnp.concatenate([x1 * cos - x2 * sin, x2 * cos + x1 * sin], axis=-1)
    return out.astype(x.dtype)


def split_heads(t, n_heads):
    b, s, w = t.shape
    return t.reshape(b, s, n_heads, w // n_heads).transpose(0, 2, 1, 3)


def merge_heads(t):
    b, h, s, d = t.shape
    return t.transpose(0, 2, 1, 3).reshape(b, s, h * d)


def chunk_causal_attention(q, k, v, scale):
    b, h, s, dk = q.shape
    dv = v.shape[-1]
    nb = s // Q_BLOCK
    q_blocks = q.reshape(b, h, nb, Q_BLOCK, dk).transpose(2, 0, 1, 3, 4)
    key_chunk = jnp.arange(s) // CHUNK

    def one_block(args):
        qi, i = args
        q_chunk = (i * Q_BLOCK + jnp.arange(Q_BLOCK)) // CHUNK
        mask = key_chunk[None, :] <= q_chunk[:, None]
        sc = jnp.einsum('bhqd,bhkd->bhqk', qi, k,
                        preferred_element_type=jnp.float32) * scale
        sc = jnp.where(mask, sc, jnp.float32(-1e30))
        p = jax.nn.softmax(sc, axis=-1).astype(v.dtype)
        return jnp.einsum('bhqk,bhkd->bhqd', p, v)

    out = lax.map(one_block, (q_blocks, jnp.arange(nb)))
    return out.transpose(1, 2, 0, 3, 4).reshape(b, h, s, dv)


def setup_inputs(seed: int = 0) -> dict:
    key = jax.random.key(seed)
    ks = jax.random.split(key, 32)

    def dense(k, shape):
        return jax.random.normal(k, (DEPTH,) + shape, jnp.float32) * (shape[0] ** -0.5)

    def gain(k, n):
        return 1.0 + 0.05 * jax.random.normal(k, (DEPTH, n), jnp.float32)

    x = jax.random.normal(ks[0], (BATCH, SEQ, D_MODEL), jnp.float32)
    offsets = jax.random.randint(ks[1], (BATCH, 1), 0, MAX_STREAM_OFFSET_CHUNKS) * CHUNK
    positions = (offsets + jnp.arange(SEQ, dtype=jnp.int32)[None, :]).astype(jnp.int32)

    return {
        "x": x,
        "positions": positions,
        "ln_mix_pre": gain(ks[2], D_MODEL),
        "w_in": dense(ks[3], (D_MODEL, IN_COLS)),
        "lambda_q1": 0.1 * jax.random.normal(ks[4], (DEPTH, DA_HEAD_DIM), jnp.float32),
        "lambda_k1": 0.1 * jax.random.normal(ks[5], (DEPTH, DA_HEAD_DIM), jnp.float32),
        "lambda_q2": 0.1 * jax.random.normal(ks[6], (DEPTH, DA_HEAD_DIM), jnp.float32),
        "lambda_k2": 0.1 * jax.random.normal(ks[7], (DEPTH, DA_HEAD_DIM), jnp.float32),
        "da_subln": gain(ks[8], DA_V_DIM),
        "q_a_norm": gain(ks[9], MLA_Q_RANK),
        "w_uq": dense(ks[10], (MLA_Q_RANK, MLA_HEADS * MLA_QK_DIM)),
        "kv_a_norm": gain(ks[11], MLA_KV_RANK),
        "w_ukv": dense(ks[12], (MLA_KV_RANK, MLA_HEADS * (MLA_NOPE_DIM + MLA_V_DIM))),
        "w_proj_a": dense(ks[13], (DA_V_WIDTH, D_MODEL)),
        "w_proj_b": dense(ks[14], (MLA_V_WIDTH, D_MODEL)),
        "w_o": dense(ks[15], (D_MODEL, D_MODEL)),
        "ln_mix_post": gain(ks[16], D_MODEL),
        "ln_ffn_pre": gain(ks[17], D_MODEL),
        "w_ffn_gate": dense(ks[18], (D_MODEL, D_FF)),
        "w_ffn_up": dense(ks[19], (D_MODEL, D_FF)),
        "w_ffn_down": dense(ks[20], (D_FF, D_MODEL)),
        "ln_ffn_post": gain(ks[21], D_MODEL),
    }


def reference(x, positions, ln_mix_pre, w_in, lambda_q1, lambda_k1, lambda_q2, lambda_k2,
              da_subln, q_a_norm, w_uq, kv_a_norm, w_ukv, w_proj_a, w_proj_b, w_o,
              ln_mix_post, ln_ffn_pre, w_ffn_gate, w_ffn_up, w_ffn_down, ln_ffn_post):
    bsz, seq, _ = x.shape
    offs = np.cumsum(IN_SPLITS)[:-1].tolist()
    for l in range(DEPTH):
        h = rms_norm(x, ln_mix_pre[l])
        z = h @ w_in[l]
        qa, ka, va, cq, ckv, k_rope, g_a, g_b = jnp.split(z, offs, axis=-1)

        lambda_init = 0.8 - 0.6 * math.exp(-0.3 * l)
        qa = split_heads(qa, DA_HEADS)
        ka = split_heads(ka, DA_HEADS)
        va = split_heads(va, DA_HEADS)
        q1 = rope(qa[..., :DA_HEAD_DIM], positions)
        q2 = rope(qa[..., DA_HEAD_DIM:], positions)
        k1 = rope(ka[..., :DA_HEAD_DIM], positions)
        k2 = rope(ka[..., DA_HEAD_DIM:], positions)
        da_scale = DA_HEAD_DIM ** -0.5
        o1 = chunk_causal_attention(q1, k1, va, da_scale)
        o2 = chunk_causal_attention(q2, k2, va, da_scale)
        lam = (jnp.exp(jnp.sum(lambda_q1[l].astype(jnp.float32) * lambda_k1[l].astype(jnp.float32)))
               - jnp.exp(jnp.sum(lambda_q2[l].astype(jnp.float32) * lambda_k2[l].astype(jnp.float32)))
               + lambda_init)
        oa = o1 - lam.astype(o1.dtype) * o2
        oa = rms_norm(oa, da_subln[l]) * (1.0 - lambda_init)
        y_a = merge_heads(oa) @ w_proj_a[l]

        cq = rms_norm(cq, q_a_norm[l])
        qb = split_heads(cq @ w_uq[l], MLA_HEADS)
        qb_nope = qb[..., :MLA_NOPE_DIM]
        qb_rope = rope(qb[..., MLA_NOPE_DIM:], positions)
        ckv = rms_norm(ckv, kv_a_norm[l])
        kv = split_heads(ckv @ w_ukv[l], MLA_HEADS)
        kb_nope = kv[..., :MLA_NOPE_DIM]
        vb = kv[..., MLA_NOPE_DIM:]
        kb_rope = rope(k_rope[:, None, :, :], positions)
        kb_rope = jnp.broadcast_to(kb_rope, (bsz, MLA_HEADS, seq, MLA_ROPE_DIM))
        qb_full = jnp.concatenate([qb_nope, qb_rope], axis=-1)
        kb_full = jnp.concatenate([kb_nope, kb_rope], axis=-1)
        ob = chunk_causal_attention(qb_full, kb_full, vb, MLA_QK_DIM ** -0.5)
        y_b = merge_heads(ob) @ w_proj_b[l]

        merged = jax.nn.sigmoid(g_a) * y_a + jax.nn.sigmoid(g_b) * y_b
        x = x + rms_norm(merged @ w_o[l], ln_mix_post[l])

        h = rms_norm(x, ln_ffn_pre[l])
        f = (jax.nn.silu(h @ w_ffn_gate[l]) * (h @ w_ffn_up[l])) @ w_ffn_down[l]
        x = x + rms_norm(f, ln_ffn_post[l])
    return x
```

```python
import functools
import math

import jax
import jax.numpy as jnp
from jax import lax
from jax.experimental import pallas as pl
from jax.experimental.pallas import tpu as pltpu

F32 = jnp.float32
BF16 = jnp.bfloat16

LANES = 128
CHUNK = 64
CHUNK_SHIFT = 6
RMS_EPS = 1e-6
ROPE_THETA = 10000.0
ROPE_DIM = 64
MASK_VALUE = -1e30

DA_HEADS = 8
DA_HEAD_DIM = 64
MLA_HEADS = 8
MLA_Q_RANK = 384
MLA_KV_RANK = 256
MLA_NOPE_DIM = 128
MLA_V_DIM = 128
MLA_QK_DIM = MLA_NOPE_DIM + ROPE_DIM
MLA_HEAD_PAD = 256

VMEM_LIMIT = 56 * 1024 * 1024


def _rms(x, w, eps=RMS_EPS):
    return x * lax.rsqrt(jnp.mean(x * x, axis=-1, keepdims=True) + eps) * w


def _rope_tile(x, cos, sin_signed, first_half):
    rot = jnp.where(first_half, pltpu.roll(x, 96, 1), pltpu.roll(x, 32, 1))
    return x * cos + rot * sin_signed


def _first_half_mask(shape):
    lane = lax.broadcasted_iota(jnp.int32, shape, 1)
    return (lane & (ROPE_DIM // 2)) == 0


def _resident(shape):
    return pl.BlockSpec(shape, lambda *_: (0,) * len(shape), pipeline_mode=pl.Buffered(1))


def _rope_table_kernel(pos_ref, invf_ref, cos_ref, sin_ref):
    ang = pos_ref[...].astype(F32) * invf_ref[...]
    first = _first_half_mask(ang.shape)
    s = jnp.sin(ang)
    cos_ref[...] = jnp.cos(ang)
    sin_ref[...] = jnp.where(first, -s, s)


def _rope_tables(pos_lanes, inv_freq_lanes, tm):
    t = pos_lanes.shape[0]
    row = pl.BlockSpec((tm, LANES), lambda i: (i, 0))
    return pl.pallas_call(
        _rope_table_kernel,
        grid=(t // tm,),
        in_specs=[row, pl.BlockSpec((1, LANES), lambda i: (0, 0))],
        out_specs=[row, row],
        out_shape=[jax.ShapeDtypeStruct((t, LANES), F32)] * 2,
        compiler_params=pltpu.CompilerParams(dimension_semantics=("parallel",)),
        name="rope_tables",
    )(pos_lanes, inv_freq_lanes)


def _in_proj_kernel(x_ref, ln_ref, w_ref, cos_ref, sin_ref,
                    q_ref, k_ref, v_ref, zm_ref, g_ref, *, d, zm_w, g_w, nc, q_scale):
    h = _rms(x_ref[...], ln_ref[...]).astype(BF16)
    cos = cos_ref[...]
    sin = sin_ref[...]
    first = _first_half_mask(cos.shape)

    def mm(c0, width):
        return jnp.dot(h, w_ref[:, c0:c0 + width], preferred_element_type=F32)

    for c in range(0, d, nc):
        z = mm(c, nc)
        for g in range(0, nc, LANES):
            r = _rope_tile(z[:, g:g + LANES], cos, sin, first) * q_scale
            q_ref[:, c + g:c + g + LANES] = r.astype(BF16)
    for c in range(0, d, nc):
        z = mm(d + c, nc)
        for g in range(0, nc, LANES):
            r = _rope_tile(z[:, g:g + LANES], cos, sin, first)
            k_ref[:, c + g:c + g + LANES] = r.astype(BF16)
    for c in range(0, d, nc):
        v_ref[:, c:c + nc] = mm(2 * d + c, nc).astype(BF16)
    zm_ref[...] = mm(3 * d, zm_w)
    for c in range(0, g_w, nc):
        g_ref[:, c:c + nc] = mm(3 * d + zm_w + c, nc)


def _in_proj(x2, ln, w_p, cos_t, sin_t, *, d, zm_w, g_w, tm, q_scale):
    t = x2.shape[0]
    row = lambda w: pl.BlockSpec((tm, w), lambda i: (i, 0))
    kern = functools.partial(_in_proj_kernel, d=d, zm_w=zm_w, g_w=g_w, nc=512, q_scale=q_scale)
    return pl.pallas_call(
        kern,
        grid=(t // tm,),
        in_specs=[row(d), _resident((1, d)), _resident(w_p.shape), row(LANES), row(LANES)],
        out_specs=[row(d), row(d), row(d), row(zm_w), row(g_w)],
        out_shape=[jax.ShapeDtypeStruct((t, d), BF16)] * 3
        + [jax.ShapeDtypeStruct((t, zm_w), F32), jax.ShapeDtypeStruct((t, g_w), F32)],
        compiler_params=pltpu.CompilerParams(
            dimension_semantics=("parallel",), vmem_limit_bytes=VMEM_LIMIT),
        name="in_proj",
    )(x2, ln, w_p, cos_t, sin_t)


def _mla_up_kernel(zm_ref, qn_ref, kvn_ref, wq_ref, wkv_ref, cos_ref, sin_ref,
                   q_ref, k_ref, v_ref, *, q_scale):
    cos = cos_ref[...]
    sin = sin_ref[...]
    first = _first_half_mask(cos.shape)
    cq = _rms(zm_ref[:, 0:MLA_Q_RANK], qn_ref[...]).astype(BF16)
    ckv = _rms(zm_ref[:, MLA_Q_RANK:MLA_Q_RANK + MLA_KV_RANK], kvn_ref[...]).astype(BF16)
    kr0 = MLA_Q_RANK + MLA_KV_RANK
    k_rope = _rope_tile(zm_ref[:, kr0:kr0 + LANES], cos, sin, first).astype(BF16)
    for hd in range(MLA_HEADS):
        c = hd * MLA_HEAD_PAD
        qh = jnp.dot(cq, wq_ref[:, c:c + MLA_HEAD_PAD], preferred_element_type=F32)
        q_ref[:, c:c + LANES] = (qh[:, :LANES] * q_scale).astype(BF16)
        q_ref[:, c + LANES:c + 2 * LANES] = (
            _rope_tile(qh[:, LANES:], cos, sin, first) * q_scale).astype(BF16)
        kvh = jnp.dot(ckv, wkv_ref[:, c:c + MLA_HEAD_PAD], preferred_element_type=F32)
        k_ref[:, c:c + LANES] = kvh[:, :LANES].astype(BF16)
        k_ref[:, c + LANES:c + 2 * LANES] = k_rope
        v_ref[:, hd * MLA_V_DIM:(hd + 1) * MLA_V_DIM] = kvh[:, LANES:].astype(BF16)


def _mla_up(zm, qn, kvn, wq_p, wkv, cos_t, sin_t, *, tm, q_scale):
    t, zm_w = zm.shape
    row = lambda w: pl.BlockSpec((tm, w), lambda i: (i, 0))
    qk_w = MLA_HEADS * MLA_HEAD_PAD
    v_w = MLA_HEADS * MLA_V_DIM
    return pl.pallas_call(
        functools.partial(_mla_up_kernel, q_scale=q_scale),
        grid=(t // tm,),
        in_specs=[row(zm_w), _resident(qn.shape), _resident(kvn.shape),
                  _resident(wq_p.shape), _resident(wkv.shape), row(LANES), row(LANES)],
        out_specs=[row(qk_w), row(qk_w), row(v_w)],
        out_shape=[jax.ShapeDtypeStruct((t, qk_w), BF16), jax.ShapeDtypeStruct((t, qk_w), BF16),
                   jax.ShapeDtypeStruct((t, v_w), BF16)],
        compiler_params=pltpu.CompilerParams(
            dimension_semantics=("parallel",), vmem_limit_bytes=VMEM_LIMIT),
        name="mla_up",
    )(zm, qn, kvn, wq_p, wkv, cos_t, sin_t)


def _flash_kernel(it_ref, jt_ref, q_ref, k_ref, v_ref, *rest, tq, tk, dual, lambda_init):
    if dual:
        lam_ref, subln_ref, o_ref, qz_sc, m_sc, l_sc, acc_sc = rest
    else:
        o_ref, m_sc, l_sc, acc_sc = rest
    p_idx = pl.program_id(2)
    i = it_ref[p_idx]
    j = jt_ref[p_idx]

    @pl.when(j == 0)
    def _init():
        if dual:
            q = q_ref[...]
            lo = lax.broadcasted_iota(jnp.int32, q.shape, 1) < DA_HEAD_DIM
            zero = jnp.zeros_like(q)
            qz_sc[0:tq, :] = jnp.where(lo, q, zero)
            qz_sc[tq:, :] = jnp.where(lo, zero, q)
        m_sc[...] = jnp.full(m_sc.shape, -jnp.inf, F32)
        l_sc[...] = jnp.zeros(l_sc.shape, F32)
        acc_sc[...] = jnp.zeros(acc_sc.shape, F32)

    def step(masked):
        q = qz_sc[...] if dual else q_ref[...]
        s = lax.dot_general(q, k_ref[...], (((1,), (1,)), ((), ())),
                            preferred_element_type=F32)
        if masked:
            r = lax.broadcasted_iota(jnp.int32, s.shape, 0) & (tq - 1)
            c = lax.broadcasted_iota(jnp.int32, s.shape, 1)
            s = jnp.where((c >> CHUNK_SHIFT) <= (r >> CHUNK_SHIFT), s, MASK_VALUE)
        m_prev = m_sc[...]
        m_new = jnp.maximum(m_prev, jnp.max(s, axis=1, keepdims=True))
        alpha = jnp.exp(m_prev - m_new)
        p = jnp.exp(s - jnp.concatenate([m_new] * (tk // LANES), axis=1))
        l_sc[...] = alpha * l_sc[...] + jnp.sum(p, axis=1, keepdims=True)
        acc_sc[...] = alpha * acc_sc[...] + jnp.dot(
            p.astype(BF16), v_ref[...], preferred_element_type=F32)
        m_sc[...] = m_new

    @pl.when(j < i)
    def _full():
        step(False)

    @pl.when(j == i)
    def _diag():
        step(True)
        o = acc_sc[...] / l_sc[...]
        if dual:
            lp = lam_ref[...]
            lam = (jnp.exp(jnp.sum(lp[0:1] * lp[1:2], axis=1, keepdims=True))
                   - jnp.exp(jnp.sum(lp[2:3] * lp[3:4], axis=1, keepdims=True))
                   + lambda_init)
            oa = o[0:tq] - lam * o[tq:]
            o_ref[...] = (_rms(oa, subln_ref[...]) * (1.0 - lambda_init)).astype(BF16)
        else:
            o_ref[...] = o.astype(BF16)


def _flash(q, k, v, extras, *, heads, dk, dv, blk, dual, lambda_init=0.0):
    bsz, seq, _ = q.shape
    assert blk % CHUNK == 0 and blk & (blk - 1) == 0 and seq % blk == 0
    nq = seq // blk
    pairs = [(i, j) for i in range(nq) for j in range(i + 1)]
    it = jnp.asarray([p[0] for p in pairs], jnp.int32)
    jt = jnp.asarray([p[1] for p in pairs], jnp.int32)
    rows = 2 * blk if dual else blk
    qspec = pl.BlockSpec((None, blk, dk), lambda b, h, p, it, jt: (b, it[p], h))
    kspec = pl.BlockSpec((None, blk, dk), lambda b, h, p, it, jt: (b, jt[p], h))
    vspec = pl.BlockSpec((None, blk, dv), lambda b, h, p, it, jt: (b, jt[p], h))
    ospec = pl.BlockSpec((None, blk, dv), lambda b, h, p, it, jt: (b, it[p], h))
    espec = [pl.BlockSpec(e.shape, lambda b, h, p, it, jt: (0, 0)) for e in extras]
    scratch = [pltpu.VMEM((rows, LANES), F32), pltpu.VMEM((rows, LANES), F32),
               pltpu.VMEM((rows, dv), F32)]
    if dual:
        scratch = [pltpu.VMEM((rows, dk), BF16)] + scratch
    kern = functools.partial(_flash_kernel, tq=blk, tk=blk, dual=dual, lambda_init=lambda_init)
    return pl.pallas_call(
        kern,
        grid_spec=pltpu.PrefetchScalarGridSpec(
            num_scalar_prefetch=2,
            grid=(bsz, heads, len(pairs)),
            in_specs=[qspec, kspec, vspec] + espec,
            out_specs=ospec,
            scratch_shapes=scratch,
        ),
        out_shape=jax.ShapeDtypeStruct((bsz, seq, heads * dv), BF16),
        compiler_params=pltpu.CompilerParams(
            dimension_semantics=("parallel", "parallel", "arbitrary"),
            vmem_limit_bytes=VMEM_LIMIT),
        name="flash_da" if dual else "flash_mla",
    )(it, jt, q, k, v, *extras)


def _post_mix_kernel(oa_ref, ob_ref, g_ref, x_ref, wa_ref, wb_ref, wo_ref, ln_ref, o_ref, *, d):
    y_a = jnp.dot(oa_ref[...], wa_ref[...], preferred_element_type=F32)
    y_b = jnp.dot(ob_ref[...], wb_ref[...], preferred_element_type=F32)
    merged = jax.nn.sigmoid(g_ref[:, 0:d]) * y_a + jax.nn.sigmoid(g_ref[:, d:2 * d]) * y_b
    m = jnp.dot(merged.astype(BF16), wo_ref[...], preferred_element_type=F32)
    o_ref[...] = x_ref[...] + _rms(m, ln_ref[...])


def _post_mix(oa, ob, g, x2, wa, wb, wo, ln, *, tm):
    t, d = x2.shape
    row = lambda w: pl.BlockSpec((tm, w), lambda i: (i, 0))
    return pl.pallas_call(
        functools.partial(_post_mix_kernel, d=d),
        grid=(t // tm,),
        in_specs=[row(d), row(d), row(2 * d), row(d), _resident(wa.shape), _resident(wb.shape),
                  _resident(wo.shape), _resident(ln.shape)],
        out_specs=row(d),
        out_shape=jax.ShapeDtypeStruct((t, d), F32),
        compiler_params=pltpu.CompilerParams(
            dimension_semantics=("parallel",), vmem_limit_bytes=VMEM_LIMIT),
        name="post_mix",
    )(oa, ob, g, x2, wa, wb, wo, ln)


def _ffn_kernel(x_ref, lnp_ref, wg_ref, wu_ref, wd_ref, lno_ref, o_ref):
    x = x_ref[...]
    h = _rms(x, lnp_ref[...]).astype(BF16)
    gate = jnp.dot(h, wg_ref[...], preferred_element_type=F32)
    up = jnp.dot(h, wu_ref[...], preferred_element_type=F32)
    a = (jax.nn.silu(gate) * up).astype(BF16)
    f = jnp.dot(a, wd_ref[...], preferred_element_type=F32)
    o_ref[...] = x + _rms(f, lno_ref[...])


def _ffn(x2, lnp, wg, wu, wd, lno, *, tm):
    t, d = x2.shape
    row = pl.BlockSpec((tm, d), lambda i: (i, 0))
    return pl.pallas_call(
        _ffn_kernel,
        grid=(t // tm,),
        in_specs=[row, _resident(lnp.shape), _resident(wg.shape), _resident(wu.shape),
                  _resident(wd.shape), _resident(lno.shape)],
        out_specs=row,
        out_shape=jax.ShapeDtypeStruct((t, d), F32),
        compiler_params=pltpu.CompilerParams(
            dimension_semantics=("parallel",), vmem_limit_bytes=VMEM_LIMIT),
        name="ffn",
    )(x2, lnp, wg, wu, wd, lno)


def _pack_w_in(w, d):
    lat = MLA_Q_RANK + MLA_KV_RANK + ROPE_DIM
    zm_w = -(-lat // LANES) * LANES
    head, latent, gates = w[:, :3 * d], w[:, 3 * d:3 * d + lat], w[:, 3 * d + lat:]
    latent = jnp.pad(latent, ((0, 0), (0, zm_w - lat)))
    return jnp.concatenate([head, latent, gates], axis=1).astype(BF16), zm_w


def _pack_w_uq(w):
    r = w.shape[0]
    w = w.reshape(r, MLA_HEADS, MLA_QK_DIM)
    w = jnp.pad(w, ((0, 0), (0, 0), (0, MLA_HEAD_PAD - MLA_QK_DIM)))
    return w.reshape(r, MLA_HEADS * MLA_HEAD_PAD).astype(BF16)


def kernel(x, positions, ln_mix_pre, w_in, lambda_q1, lambda_k1, lambda_q2, lambda_k2, da_subln,
           q_a_norm, w_uq, kv_a_norm, w_ukv, w_proj_a, w_proj_b, w_o, ln_mix_post, ln_ffn_pre,
           w_ffn_gate, w_ffn_up, w_ffn_down, ln_ffn_post):
    bsz, seq, d = x.shape
    t = bsz * seq
    depth = w_in.shape[0]
    tm = 512

    half = ROPE_DIM // 2
    inv_freq = 1.0 / (ROPE_THETA ** (jnp.arange(half, dtype=F32) * (2.0 / ROPE_DIM)))
    inv_freq_lanes = jnp.tile(inv_freq, LANES // half).reshape(1, LANES)
    pos_lanes = jnp.broadcast_to(positions.reshape(t, 1), (t, LANES))
    cos_t, sin_t = _rope_tables(pos_lanes, inv_freq_lanes, 1024)

    x2 = x.reshape(t, d)
    for l in range(depth):
        lambda_init = 0.8 - 0.6 * math.exp(-0.3 * l)
        w_p, zm_w = _pack_w_in(w_in[l], d)
        q_da, k_da, v_da, zm, gates = _in_proj(
            x2, ln_mix_pre[l].reshape(1, d), w_p, cos_t, sin_t,
            d=d, zm_w=zm_w, g_w=2 * d, tm=tm, q_scale=DA_HEAD_DIM ** -0.5)
        q_b, k_b, v_b = _mla_up(
            zm, q_a_norm[l].reshape(1, -1), kv_a_norm[l].reshape(1, -1),
            _pack_w_uq(w_uq[l]), w_ukv[l].astype(BF16), cos_t, sin_t,
            tm=tm, q_scale=MLA_QK_DIM ** -0.5)

        lam_p = jnp.stack([lambda_q1[l], lambda_k1[l], lambda_q2[l], lambda_k2[l]]).astype(F32)
        lam_p = jnp.pad(lam_p, ((0, 0), (0, LANES - DA_HEAD_DIM)))
        shp = lambda a: a.reshape(bsz, seq, a.shape[-1])
        oa = _flash(shp(q_da), shp(k_da), shp(v_da), [lam_p, da_subln[l].reshape(1, -1)],
                    heads=DA_HEADS, dk=2 * DA_HEAD_DIM, dv=2 * DA_HEAD_DIM, blk=512, dual=True,
                    lambda_init=lambda_init)
        ob = _flash(shp(q_b), shp(k_b), shp(v_b), [], heads=MLA_HEADS, dk=MLA_HEAD_PAD,
                    dv=MLA_V_DIM, blk=512, dual=False)

        x2 = _post_mix(oa.reshape(t, d), ob.reshape(t, d), gates, x2,
                       w_proj_a[l].astype(BF16), w_proj_b[l].astype(BF16), w_o[l].astype(BF16),
                       ln_mix_post[l].reshape(1, d), tm=tm)
        x2 = _ffn(x2, ln_ffn_pre[l].reshape(1, d), w_ffn_gate[l].astype(BF16),
                  w_ffn_up[l].astype(BF16), w_ffn_down[l].astype(BF16),
                  ln_ffn_post[l].reshape(1, d), tm=tm)
    return x2.reshape(bsz, seq, d)
```

```python
import functools
import math

import jax
import jax.numpy as jnp
from jax import lax
from jax.experimental import pallas as pl
from jax.experimental.pallas import tpu as pltpu

F32 = jnp.float32
BF16 = jnp.bfloat16

LANES = 128
CHUNK = 64
CHUNK_SHIFT = 6
RMS_EPS = 1e-6
ROPE_THETA = 10000.0
ROPE_DIM = 64
MASK_VALUE = -1e30

DA_HEADS = 8
DA_HEAD_DIM = 64
MLA_HEADS = 8
MLA_Q_RANK = 384
MLA_KV_RANK = 256
MLA_NOPE_DIM = 128
MLA_V_DIM = 128
MLA_QK_DIM = MLA_NOPE_DIM + ROPE_DIM
MLA_HEAD_PAD = 256

VMEM_LIMIT = 56 * 1024 * 1024


def _rms(x, w, eps=RMS_EPS):
    return x * lax.rsqrt(jnp.mean(x * x, axis=-1, keepdims=True) + eps) * w


def _rope_tile(x, cos, sin_signed, first_half):
    rot = jnp.where(first_half, pltpu.roll(x, 96, 1), pltpu.roll(x, 32, 1))
    return x * cos + rot * sin_signed


def _first_half_mask(shape):
    lane = lax.broadcasted_iota(jnp.int32, shape, 1)
    return (lane & (ROPE_DIM // 2)) == 0


def _resident(shape):
    return pl.BlockSpec(shape, lambda *_: (0,) * len(shape), pipeline_mode=pl.Buffered(1))


def _rope_table_kernel(pos_ref, invf_ref, cos_ref, sin_ref):
    ang = pos_ref[...].astype(F32) * invf_ref[...]
    first = _first_half_mask(ang.shape)
    s = jnp.sin(ang)
    cos_ref[...] = jnp.cos(ang)
    sin_ref[...] = jnp.where(first, -s, s)


def _rope_tables(pos_lanes, inv_freq_lanes, tm):
    t = pos_lanes.shape[0]
    row = pl.BlockSpec((tm, LANES), lambda i: (i, 0))
    return pl.pallas_call(
        _rope_table_kernel,
        grid=(t // tm,),
        in_specs=[row, pl.BlockSpec((1, LANES), lambda i: (0, 0))],
        out_specs=[row, row],
        out_shape=[jax.ShapeDtypeStruct((t, LANES), F32)] * 2,
        compiler_params=pltpu.CompilerParams(dimension_semantics=("parallel",)),
        name="rope_tables",
    )(pos_lanes, inv_freq_lanes)


def _in_proj_kernel(x_ref, ln_ref, w_ref, cos_ref, sin_ref,
                    q_ref, k_ref, v_ref, zm_ref, g_ref, *, d, zm_w, g_w, nc, q_scale):
    h = _rms(x_ref[...], ln_ref[...]).astype(BF16)
    cos = cos_ref[...]
    sin = sin_ref[...]
    first = _first_half_mask(cos.shape)

    def mm(c0, width):
        return jnp.dot(h, w_ref[:, c0:c0 + width], preferred_element_type=F32)

    for c in range(0, d, nc):
        z = mm(c, nc)
        for g in range(0, nc, LANES):
            r = _rope_tile(z[:, g:g + LANES], cos, sin, first) * q_scale
            q_ref[(c + g) // LANES] = r.astype(BF16)
    for c in range(0, d, nc):
        z = mm(d + c, nc)
        for g in range(0, nc, LANES):
            r = _rope_tile(z[:, g:g + LANES], cos, sin, first)
            k_ref[(c + g) // LANES] = r.astype(BF16)
    for c in range(0, d, nc):
        z = mm(2 * d + c, nc)
        for g in range(0, nc, LANES):
            v_ref[(c + g) // LANES] = z[:, g:g + LANES].astype(BF16)
    zm_ref[...] = mm(3 * d, zm_w)
    for c in range(0, g_w, nc):
        g_ref[:, c:c + nc] = mm(3 * d + zm_w + c, nc)


def _head_major_spec(heads, tm, w, seq):
    per_seq = seq // tm
    return pl.BlockSpec((None, heads, tm, w), lambda i: (i // per_seq, 0, i % per_seq, 0))


def _in_proj(x2, ln, w_p, cos_t, sin_t, *, bsz, seq, d, zm_w, g_w, tm, q_scale):
    t = x2.shape[0]
    row = lambda w: pl.BlockSpec((tm, w), lambda i: (i, 0))
    heads = d // LANES
    hm = _head_major_spec(heads, tm, LANES, seq)
    kern = functools.partial(_in_proj_kernel, d=d, zm_w=zm_w, g_w=g_w, nc=512, q_scale=q_scale)
    return pl.pallas_call(
        kern,
        grid=(t // tm,),
        in_specs=[row(d), _resident((1, d)), _resident(w_p.shape), row(LANES), row(LANES)],
        out_specs=[hm, hm, hm, row(zm_w), row(g_w)],
        out_shape=[jax.ShapeDtypeStruct((bsz, heads, seq, LANES), BF16)] * 3
        + [jax.ShapeDtypeStruct((t, zm_w), F32), jax.ShapeDtypeStruct((t, g_w), F32)],
        compiler_params=pltpu.CompilerParams(
            dimension_semantics=("parallel",), vmem_limit_bytes=VMEM_LIMIT),
        name="in_proj",
    )(x2, ln, w_p, cos_t, sin_t)


def _mla_up_kernel(zm_ref, qn_ref, kvn_ref, wq_ref, wkv_ref, cos_ref, sin_ref,
                   q_ref, k_ref, v_ref, *, q_scale):
    cos = cos_ref[...]
    sin = sin_ref[...]
    first = _first_half_mask(cos.shape)
    cq = _rms(zm_ref[:, 0:MLA_Q_RANK], qn_ref[...]).astype(BF16)
    ckv = _rms(zm_ref[:, MLA_Q_RANK:MLA_Q_RANK + MLA_KV_RANK], kvn_ref[...]).astype(BF16)
    kr0 = MLA_Q_RANK + MLA_KV_RANK
    k_rope = _rope_tile(zm_ref[:, kr0:kr0 + LANES], cos, sin, first).astype(BF16)
    for hd in range(MLA_HEADS):
        c = hd * MLA_HEAD_PAD
        qh = jnp.dot(cq, wq_ref[:, c:c + MLA_HEAD_PAD], preferred_element_type=F32)
        q_ref[hd, :, 0:LANES] = (qh[:, :LANES] * q_scale).astype(BF16)
        q_ref[hd, :, LANES:2 * LANES] = (
            _rope_tile(qh[:, LANES:], cos, sin, first) * q_scale).astype(BF16)
        kvh = jnp.dot(ckv, wkv_ref[:, c:c + MLA_HEAD_PAD], preferred_element_type=F32)
        k_ref[hd, :, 0:LANES] = kvh[:, :LANES].astype(BF16)
        k_ref[hd, :, LANES:2 * LANES] = k_rope
        v_ref[hd] = kvh[:, LANES:].astype(BF16)


def _mla_up(zm, qn, kvn, wq_p, wkv, cos_t, sin_t, *, bsz, seq, tm, q_scale):
    t, zm_w = zm.shape
    row = lambda w: pl.BlockSpec((tm, w), lambda i: (i, 0))
    qk = _head_major_spec(MLA_HEADS, tm, MLA_HEAD_PAD, seq)
    vv = _head_major_spec(MLA_HEADS, tm, MLA_V_DIM, seq)
    return pl.pallas_call(
        functools.partial(_mla_up_kernel, q_scale=q_scale),
        grid=(t // tm,),
        in_specs=[row(zm_w), _resident(qn.shape), _resident(kvn.shape),
                  _resident(wq_p.shape), _resident(wkv.shape), row(LANES), row(LANES)],
        out_specs=[qk, qk, vv],
        out_shape=[jax.ShapeDtypeStruct((bsz, MLA_HEADS, seq, MLA_HEAD_PAD), BF16)] * 2
        + [jax.ShapeDtypeStruct((bsz, MLA_HEADS, seq, MLA_V_DIM), BF16)],
        compiler_params=pltpu.CompilerParams(
            dimension_semantics=("parallel",), vmem_limit_bytes=VMEM_LIMIT),
        name="mla_up",
    )(zm, qn, kvn, wq_p, wkv, cos_t, sin_t)


def _flash_kernel(it_ref, jt_ref, q_ref, k_ref, v_ref, *rest, tq, tk, dual, lambda_init):
    if dual:
        lam_ref, subln_ref, o_ref, qz_sc, m_sc, l_sc, acc_sc = rest
    else:
        o_ref, m_sc, l_sc, acc_sc = rest
    p_idx = pl.program_id(2)
    i = it_ref[p_idx]
    j = jt_ref[p_idx]

    @pl.when(j == 0)
    def _init():
        if dual:
            q = q_ref[...]
            lo = lax.broadcasted_iota(jnp.int32, q.shape, 1) < DA_HEAD_DIM
            zero = jnp.zeros_like(q)
            qz_sc[0:tq, :] = jnp.where(lo, q, zero)
            qz_sc[tq:, :] = jnp.where(lo, zero, q)
        m_sc[...] = jnp.full(m_sc.shape, -jnp.inf, F32)
        l_sc[...] = jnp.zeros(l_sc.shape, F32)
        acc_sc[...] = jnp.zeros(acc_sc.shape, F32)

    def step(masked):
        q = qz_sc[...] if dual else q_ref[...]
        s = lax.dot_general(q, k_ref[...], (((1,), (1,)), ((), ())),
                            preferred_element_type=F32)
        if masked:
            r = lax.broadcasted_iota(jnp.int32, s.shape, 0) & (tq - 1)
            c = lax.broadcasted_iota(jnp.int32, s.shape, 1)
            s = jnp.where((c >> CHUNK_SHIFT) <= (r >> CHUNK_SHIFT), s, MASK_VALUE)
        m_prev = m_sc[...]
        m_new = jnp.maximum(m_prev, jnp.max(s, axis=1, keepdims=True))
        alpha = jnp.exp(m_prev - m_new)
        p = jnp.exp(s - jnp.concatenate([m_new] * (tk // LANES), axis=1))
        l_sc[...] = alpha * l_sc[...] + jnp.sum(p, axis=1, keepdims=True)
        acc_sc[...] = alpha * acc_sc[...] + jnp.dot(
            p.astype(BF16), v_ref[...], preferred_element_type=F32)
        m_sc[...] = m_new

    @pl.when(j < i)
    def _full():
        step(False)

    @pl.when(j == i)
    def _diag():
        step(True)
        o = acc_sc[...] / l_sc[...]
        if dual:
            lp = lam_ref[...]
            lam = (jnp.exp(jnp.sum(lp[0:1] * lp[1:2], axis=1, keepdims=True))
                   - jnp.exp(jnp.sum(lp[2:3] * lp[3:4], axis=1, keepdims=True))
                   + lambda_init)
            oa = o[0:tq] - lam * o[tq:]
            o_ref[...] = (_rms(oa, subln_ref[...]) * (1.0 - lambda_init)).astype(BF16)
        else:
            o_ref[...] = o.astype(BF16)


def _flash(q, k, v, extras, *, blk, dual, lambda_init=0.0):
    bsz, heads, seq, dk = q.shape
    dv = v.shape[-1]
    assert blk % CHUNK == 0 and blk & (blk - 1) == 0 and seq % blk == 0
    nq = seq // blk
    pairs = [(i, j) for i in range(nq) for j in range(i + 1)]
    it = jnp.asarray([p[0] for p in pairs], jnp.int32)
    jt = jnp.asarray([p[1] for p in pairs], jnp.int32)
    rows = 2 * blk if dual else blk
    qspec = pl.BlockSpec((None, None, blk, dk), lambda b, h, p, it, jt: (b, h, it[p], 0))
    kspec = pl.BlockSpec((None, None, blk, dk), lambda b, h, p, it, jt: (b, h, jt[p], 0))
    vspec = pl.BlockSpec((None, None, blk, dv), lambda b, h, p, it, jt: (b, h, jt[p], 0))
    ospec = pl.BlockSpec((None, blk, dv), lambda b, h, p, it, jt: (b, it[p], h))
    espec = [pl.BlockSpec(e.shape, lambda b, h, p, it, jt: (0, 0)) for e in extras]
    scratch = [pltpu.VMEM((rows, LANES), F32), pltpu.VMEM((rows, LANES), F32),
               pltpu.VMEM((rows, dv), F32)]
    if dual:
        scratch = [pltpu.VMEM((rows, dk), BF16)] + scratch
    kern = functools.partial(_flash_kernel, tq=blk, tk=blk, dual=dual, lambda_init=lambda_init)
    return pl.pallas_call(
        kern,
        grid_spec=pltpu.PrefetchScalarGridSpec(
            num_scalar_prefetch=2,
            grid=(bsz, heads, len(pairs)),
            in_specs=[qspec, kspec, vspec] + espec,
            out_specs=ospec,
            scratch_shapes=scratch,
        ),
        out_shape=jax.ShapeDtypeStruct((bsz, seq, heads * dv), BF16),
        compiler_params=pltpu.CompilerParams(
            dimension_semantics=("parallel", "parallel", "arbitrary"),
            vmem_limit_bytes=VMEM_LIMIT),
        name="flash_da" if dual else "flash_mla",
    )(it, jt, q, k, v, *extras)


def _post_mix_kernel(oa_ref, ob_ref, g_ref, x_ref, wa_ref, wb_ref, wo_ref, ln_ref, o_ref, *, d):
    y_a = jnp.dot(oa_ref[...], wa_ref[...], preferred_element_type=F32)
    y_b = jnp.dot(ob_ref[...], wb_ref[...], preferred_element_type=F32)
    merged = jax.nn.sigmoid(g_ref[:, 0:d]) * y_a + jax.nn.sigmoid(g_ref[:, d:2 * d]) * y_b
    m = jnp.dot(merged.astype(BF16), wo_ref[...], preferred_element_type=F32)
    o_ref[...] = x_ref[...] + _rms(m, ln_ref[...])


def _post_mix(oa, ob, g, x2, wa, wb, wo, ln, *, tm):
    t, d = x2.shape
    row = lambda w: pl.BlockSpec((tm, w), lambda i: (i, 0))
    return pl.pallas_call(
        functools.partial(_post_mix_kernel, d=d),
        grid=(t // tm,),
        in_specs=[row(d), row(d), row(2 * d), row(d), _resident(wa.shape), _resident(wb.shape),
                  _resident(wo.shape), _resident(ln.shape)],
        out_specs=row(d),
        out_shape=jax.ShapeDtypeStruct((t, d), F32),
        compiler_params=pltpu.CompilerParams(
            dimension_semantics=("parallel",), vmem_limit_bytes=VMEM_LIMIT),
        name="post_mix",
    )(oa, ob, g, x2, wa, wb, wo, ln)


def _ffn_kernel(x_ref, lnp_ref, wg_ref, wu_ref, wd_ref, lno_ref, o_ref):
    x = x_ref[...]
    h = _rms(x, lnp_ref[...]).astype(BF16)
    gate = jnp.dot(h, wg_ref[...], preferred_element_type=F32)
    up = jnp.dot(h, wu_ref[...], preferred_element_type=F32)
    a = (jax.nn.silu(gate) * up).astype(BF16)
    f = jnp.dot(a, wd_ref[...], preferred_element_type=F32)
    o_ref[...] = x + _rms(f, lno_ref[...])


def _ffn(x2, lnp, wg, wu, wd, lno, *, tm):
    t, d = x2.shape
    row = pl.BlockSpec((tm, d), lambda i: (i, 0))
    return pl.pallas_call(
        _ffn_kernel,
        grid=(t // tm,),
        in_specs=[row, _resident(lnp.shape), _resident(wg.shape), _resident(wu.shape),
                  _resident(wd.shape), _resident(lno.shape)],
        out_specs=row,
        out_shape=jax.ShapeDtypeStruct((t, d), F32),
        compiler_params=pltpu.CompilerParams(
            dimension_semantics=("parallel",), vmem_limit_bytes=VMEM_LIMIT),
        name="ffn",
    )(x2, lnp, wg, wu, wd, lno)


def _pack_w_in(w, d):
    lat = MLA_Q_RANK + MLA_KV_RANK + ROPE_DIM
    zm_w = -(-lat // LANES) * LANES
    head, latent, gates = w[:, :3 * d], w[:, 3 * d:3 * d + lat], w[:, 3 * d + lat:]
    latent = jnp.pad(latent, ((0, 0), (0, zm_w - lat)))
    return jnp.concatenate([head, latent, gates], axis=1).astype(BF16), zm_w


def _pack_w_uq(w):
    r = w.shape[0]
    w = w.reshape(r, MLA_HEADS, MLA_QK_DIM)
    w = jnp.pad(w, ((0, 0), (0, 0), (0, MLA_HEAD_PAD - MLA_QK_DIM)))
    return w.reshape(r, MLA_HEADS * MLA_HEAD_PAD).astype(BF16)


def kernel(x, positions, ln_mix_pre, w_in, lambda_q1, lambda_k1, lambda_q2, lambda_k2, da_subln,
           q_a_norm, w_uq, kv_a_norm, w_ukv, w_proj_a, w_proj_b, w_o, ln_mix_post, ln_ffn_pre,
           w_ffn_gate, w_ffn_up, w_ffn_down, ln_ffn_post):
    bsz, seq, d = x.shape
    t = bsz * seq
    depth = w_in.shape[0]
    tm = 512

    half = ROPE_DIM // 2
    inv_freq = 1.0 / (ROPE_THETA ** (jnp.arange(half, dtype=F32) * (2.0 / ROPE_DIM)))
    inv_freq_lanes = jnp.tile(inv_freq, LANES // half).reshape(1, LANES)
    pos_lanes = jnp.broadcast_to(positions.reshape(t, 1), (t, LANES))
    cos_t, sin_t = _rope_tables(pos_lanes, inv_freq_lanes, 1024)

    x2 = x.reshape(t, d)
    for l in range(depth):
        lambda_init = 0.8 - 0.6 * math.exp(-0.3 * l)
        w_p, zm_w = _pack_w_in(w_in[l], d)
        q_da, k_da, v_da, zm, gates = _in_proj(
            x2, ln_mix_pre[l].reshape(1, d), w_p, cos_t, sin_t,
            bsz=bsz, seq=seq, d=d, zm_w=zm_w, g_w=2 * d, tm=tm, q_scale=DA_HEAD_DIM ** -0.5)
        q_b, k_b, v_b = _mla_up(
            zm, q_a_norm[l].reshape(1, -1), kv_a_norm[l].reshape(1, -1),
            _pack_w_uq(w_uq[l]), w_ukv[l].astype(BF16), cos_t, sin_t,
            bsz=bsz, seq=seq, tm=tm, q_scale=MLA_QK_DIM ** -0.5)

        lam_p = jnp.stack([lambda_q1[l], lambda_k1[l], lambda_q2[l], lambda_k2[l]]).astype(F32)
        lam_p = jnp.pad(lam_p, ((0, 0), (0, LANES - DA_HEAD_DIM)))
        oa = _flash(q_da, k_da, v_da, [lam_p, da_subln[l].reshape(1, -1)], blk=512, dual=True,
                    lambda_init=lambda_init)
        ob = _flash(q_b, k_b, v_b, [], blk=512, dual=False)

        x2 = _post_mix(oa.reshape(t, d), ob.reshape(t, d), gates, x2,
                       w_proj_a[l].astype(BF16), w_proj_b[l].astype(BF16), w_o[l].astype(BF16),
                       ln_mix_post[l].reshape(1, d), tm=tm)
        x2 = _ffn(x2, ln_ffn_pre[l].reshape(1, d), w_ffn_gate[l].astype(BF16),
                  w_ffn_up[l].astype(BF16), w_ffn_down[l].astype(BF16),
                  ln_ffn_post[l].reshape(1, d), tm=tm)
    return x2.reshape(bsz, seq, d)
```

```python
import functools
import math

import jax
import jax.numpy as jnp
from jax import lax
from jax.experimental import pallas as pl
from jax.experimental.pallas import tpu as pltpu

F32 = jnp.float32
BF16 = jnp.bfloat16

LANES = 128
CHUNK = 64
CHUNK_SHIFT = 6
RMS_EPS = 1e-6
ROPE_THETA = 10000.0
ROPE_DIM = 64
MASK_VALUE = -1e30
LOG2E = math.log2(math.e)

DA_HEADS = 8
DA_HEAD_DIM = 64
MLA_HEADS = 8
MLA_Q_RANK = 384
MLA_KV_RANK = 256
MLA_NOPE_DIM = 128
MLA_V_DIM = 128
MLA_QK_DIM = MLA_NOPE_DIM + ROPE_DIM
MLA_HEAD_PAD = 256

VMEM_LIMIT = 56 * 1024 * 1024


def _rms(x, w, eps=RMS_EPS):
    return x * lax.rsqrt(jnp.mean(x * x, axis=-1, keepdims=True) + eps) * w


def _rope_tile(x, cos, sin_signed, first_half):
    rot = jnp.where(first_half, pltpu.roll(x, 96, 1), pltpu.roll(x, 32, 1))
    return x * cos + rot * sin_signed


def _first_half_mask(shape):
    lane = lax.broadcasted_iota(jnp.int32, shape, 1)
    return (lane & (ROPE_DIM // 2)) == 0


def _resident(shape):
    return pl.BlockSpec(shape, lambda *_: (0,) * len(shape), pipeline_mode=pl.Buffered(1))


def _rope_table_kernel(pos_ref, invf_ref, cos_ref, sin_ref):
    ang = pos_ref[...].astype(F32) * invf_ref[...]
    first = _first_half_mask(ang.shape)
    s = jnp.sin(ang)
    cos_ref[...] = jnp.cos(ang)
    sin_ref[...] = jnp.where(first, -s, s)


def _rope_tables(pos_lanes, inv_freq_lanes, tm):
    t = pos_lanes.shape[0]
    row = pl.BlockSpec((tm, LANES), lambda i: (i, 0))
    return pl.pallas_call(
        _rope_table_kernel,
        grid=(t // tm,),
        in_specs=[row, pl.BlockSpec((1, LANES), lambda i: (0, 0))],
        out_specs=[row, row],
        out_shape=[jax.ShapeDtypeStruct((t, LANES), F32)] * 2,
        compiler_params=pltpu.CompilerParams(dimension_semantics=("parallel",)),
        name="rope_tables",
    )(pos_lanes, inv_freq_lanes)


def _in_proj_kernel(x_ref, ln_ref, w_ref, cos_ref, sin_ref,
                    q_ref, k_ref, v_ref, zm_ref, g_ref, *, d, zm_w, g_w, nc, q_scale):
    h = _rms(x_ref[...], ln_ref[...]).astype(BF16)
    cos = cos_ref[...]
    sin = sin_ref[...]
    first = _first_half_mask(cos.shape)

    def mm(c0, width):
        return jnp.dot(h, w_ref[:, c0:c0 + width], preferred_element_type=F32)

    for c in range(0, d, nc):
        z = mm(c, nc)
        for g in range(0, nc, LANES):
            r = _rope_tile(z[:, g:g + LANES], cos, sin, first) * q_scale
            q_ref[(c + g) // LANES] = r.astype(BF16)
    for c in range(0, d, nc):
        z = mm(d + c, nc)
        for g in range(0, nc, LANES):
            r = _rope_tile(z[:, g:g + LANES], cos, sin, first)
            k_ref[(c + g) // LANES] = r.astype(BF16)
    for c in range(0, d, nc):
        z = mm(2 * d + c, nc)
        for g in range(0, nc, LANES):
            v_ref[(c + g) // LANES] = z[:, g:g + LANES].astype(BF16)
    zm_ref[...] = mm(3 * d, zm_w)
    for c in range(0, g_w, nc):
        g_ref[:, c:c + nc] = mm(3 * d + zm_w + c, nc)


def _head_major_spec(heads, tm, w, seq):
    per_seq = seq // tm
    return pl.BlockSpec((None, heads, tm, w), lambda i: (i // per_seq, 0, i % per_seq, 0))


def _in_proj(x2, ln, w_p, cos_t, sin_t, *, bsz, seq, d, zm_w, g_w, tm, q_scale):
    t = x2.shape[0]
    row = lambda w: pl.BlockSpec((tm, w), lambda i: (i, 0))
    heads = d // LANES
    hm = _head_major_spec(heads, tm, LANES, seq)
    kern = functools.partial(_in_proj_kernel, d=d, zm_w=zm_w, g_w=g_w, nc=512, q_scale=q_scale)
    return pl.pallas_call(
        kern,
        grid=(t // tm,),
        in_specs=[row(d), _resident((1, d)), _resident(w_p.shape), row(LANES), row(LANES)],
        out_specs=[hm, hm, hm, row(zm_w), row(g_w)],
        out_shape=[jax.ShapeDtypeStruct((bsz, heads, seq, LANES), BF16)] * 3
        + [jax.ShapeDtypeStruct((t, zm_w), F32), jax.ShapeDtypeStruct((t, g_w), F32)],
        compiler_params=pltpu.CompilerParams(
            dimension_semantics=("parallel",), vmem_limit_bytes=VMEM_LIMIT),
        name="in_proj",
    )(x2, ln, w_p, cos_t, sin_t)


def _mla_up_kernel(zm_ref, qn_ref, kvn_ref, wq_ref, wkv_ref, cos_ref, sin_ref,
                   q_ref, k_ref, v_ref, *, q_scale):
    cos = cos_ref[...]
    sin = sin_ref[...]
    first = _first_half_mask(cos.shape)
    cq = _rms(zm_ref[:, 0:MLA_Q_RANK], qn_ref[...]).astype(BF16)
    ckv = _rms(zm_ref[:, MLA_Q_RANK:MLA_Q_RANK + MLA_KV_RANK], kvn_ref[...]).astype(BF16)
    kr0 = MLA_Q_RANK + MLA_KV_RANK
    k_rope = _rope_tile(zm_ref[:, kr0:kr0 + LANES], cos, sin, first).astype(BF16)
    for hd in range(MLA_HEADS):
        c = hd * MLA_HEAD_PAD
        qh = jnp.dot(cq, wq_ref[:, c:c + MLA_HEAD_PAD], preferred_element_type=F32)
        q_ref[hd, :, 0:LANES] = (qh[:, :LANES] * q_scale).astype(BF16)
        q_ref[hd, :, LANES:2 * LANES] = (
            _rope_tile(qh[:, LANES:], cos, sin, first) * q_scale).astype(BF16)
        kvh = jnp.dot(ckv, wkv_ref[:, c:c + MLA_HEAD_PAD], preferred_element_type=F32)
        k_ref[hd, :, 0:LANES] = kvh[:, :LANES].astype(BF16)
        k_ref[hd, :, LANES:2 * LANES] = k_rope
        v_ref[hd] = kvh[:, LANES:].astype(BF16)


def _mla_up(zm, qn, kvn, wq_p, wkv, cos_t, sin_t, *, bsz, seq, tm, q_scale):
    t, zm_w = zm.shape
    row = lambda w: pl.BlockSpec((tm, w), lambda i: (i, 0))
    qk = _head_major_spec(MLA_HEADS, tm, MLA_HEAD_PAD, seq)
    vv = _head_major_spec(MLA_HEADS, tm, MLA_V_DIM, seq)
    return pl.pallas_call(
        functools.partial(_mla_up_kernel, q_scale=q_scale),
        grid=(t // tm,),
        in_specs=[row(zm_w), _resident(qn.shape), _resident(kvn.shape),
                  _resident(wq_p.shape), _resident(wkv.shape), row(LANES), row(LANES)],
        out_specs=[qk, qk, vv],
        out_shape=[jax.ShapeDtypeStruct((bsz, MLA_HEADS, seq, MLA_HEAD_PAD), BF16)] * 2
        + [jax.ShapeDtypeStruct((bsz, MLA_HEADS, seq, MLA_V_DIM), BF16)],
        compiler_params=pltpu.CompilerParams(
            dimension_semantics=("parallel",), vmem_limit_bytes=VMEM_LIMIT),
        name="mla_up",
    )(zm, qn, kvn, wq_p, wkv, cos_t, sin_t)


def _flash_kernel(q_ref, k_ref, v_ref, *rest, tq, dual, lambda_init):
    if dual:
        lam_ref, subln_ref, o_ref, qz_sc, sa_sc, sb_sc, m_sc, l_sc, acc_sc = rest
    else:
        o_ref, sa_sc, sb_sc, m_sc, l_sc, acc_sc = rest
    i = pl.program_id(2)

    if dual:
        q_in = q_ref[...]
        lo = lax.broadcasted_iota(jnp.int32, q_in.shape, 1) < DA_HEAD_DIM
        zero = jnp.zeros_like(q_in)
        qz_sc[0:tq, :] = jnp.where(lo, q_in, zero)
        qz_sc[tq:, :] = jnp.where(lo, zero, q_in)
    m_sc[...] = jnp.full(m_sc.shape, -jnp.inf, F32)
    l_sc[...] = jnp.zeros(l_sc.shape, F32)
    acc_sc[...] = jnp.zeros(acc_sc.shape, F32)

    def keys(ref, j):
        return ref[pl.ds(pl.multiple_of(j * tq, tq), tq), :]

    def qk(s_ref, j):
        q = qz_sc[...] if dual else q_ref[...]
        s_ref[...] = lax.dot_general(q, keys(k_ref, j), (((1,), (1,)), ((), ())),
                                     preferred_element_type=F32)

    def softmax_pv(s_ref, j, masked):
        s = s_ref[...]
        if masked:
            r = lax.broadcasted_iota(jnp.int32, s.shape, 0) & (tq - 1)
            c = lax.broadcasted_iota(jnp.int32, s.shape, 1)
            s = jnp.where((c >> CHUNK_SHIFT) <= (r >> CHUNK_SHIFT), s, MASK_VALUE)
        m_prev = m_sc[...]
        m_new = jnp.maximum(m_prev, jnp.max(s, axis=1, keepdims=True))
        alpha = jnp.exp2(m_prev - m_new)
        p = jnp.exp2(s - jnp.concatenate([m_new] * (tq // LANES), axis=1))
        l_sc[...] = alpha * l_sc[...] + jnp.sum(p, axis=1, keepdims=True)
        acc_sc[...] = alpha * acc_sc[...] + jnp.dot(
            p.astype(BF16), keys(v_ref, j), preferred_element_type=F32)
        m_sc[...] = m_new

    qk(sa_sc, 0)

    def pair(t, carry):
        qk(sb_sc, 2 * t + 1)
        softmax_pv(sa_sc, 2 * t, False)
        qk(sa_sc, 2 * t + 2)
        softmax_pv(sb_sc, 2 * t + 1, False)
        return carry

    lax.fori_loop(0, i // 2, pair, 0)

    @pl.when(i % 2 == 0)
    def _even_tail():
        softmax_pv(sa_sc, i, True)

    @pl.when(i % 2 == 1)
    def _odd_tail():
        qk(sb_sc, i)
        softmax_pv(sa_sc, i - 1, False)
        softmax_pv(sb_sc, i, True)

    o = acc_sc[...] / l_sc[...]
    if dual:
        lp = lam_ref[...]
        lam = (jnp.exp(jnp.sum(lp[0:1] * lp[1:2], axis=1, keepdims=True))
               - jnp.exp(jnp.sum(lp[2:3] * lp[3:4], axis=1, keepdims=True))
               + lambda_init)
        oa = o[0:tq] - lam * o[tq:]
        o_ref[...] = (_rms(oa, subln_ref[...]) * (1.0 - lambda_init)).astype(BF16)
    else:
        o_ref[...] = o.astype(BF16)


def _flash(q, k, v, extras, *, blk, dual, lambda_init=0.0):
    bsz, heads, seq, dk = q.shape
    dv = v.shape[-1]
    assert blk % CHUNK == 0 and blk & (blk - 1) == 0 and seq % blk == 0
    rows = 2 * blk if dual else blk
    qspec = pl.BlockSpec((None, None, blk, dk), lambda b, h, i: (b, h, i, 0))
    kspec = pl.BlockSpec((None, None, seq, dk), lambda b, h, i: (b, h, 0, 0))
    vspec = pl.BlockSpec((None, None, seq, dv), lambda b, h, i: (b, h, 0, 0))
    ospec = pl.BlockSpec((None, blk, dv), lambda b, h, i: (b, i, h))
    espec = [pl.BlockSpec(e.shape, lambda b, h, i: (0, 0)) for e in extras]
    scratch = [pltpu.VMEM((rows, blk), F32), pltpu.VMEM((rows, blk), F32),
               pltpu.VMEM((rows, LANES), F32), pltpu.VMEM((rows, LANES), F32),
               pltpu.VMEM((rows, dv), F32)]
    if dual:
        scratch = [pltpu.VMEM((rows, dk), BF16)] + scratch
    kern = functools.partial(_flash_kernel, tq=blk, dual=dual, lambda_init=lambda_init)
    return pl.pallas_call(
        kern,
        grid=(bsz, heads, seq // blk),
        in_specs=[qspec, kspec, vspec] + espec,
        out_specs=ospec,
        scratch_shapes=scratch,
        out_shape=jax.ShapeDtypeStruct((bsz, seq, heads * dv), BF16),
        compiler_params=pltpu.CompilerParams(
            dimension_semantics=("parallel", "parallel", "arbitrary"),
            vmem_limit_bytes=VMEM_LIMIT),
        name="flash_da" if dual else "flash_mla",
    )(q, k, v, *extras)


def _post_mix_kernel(oa_ref, ob_ref, g_ref, x_ref, wa_ref, wb_ref, wo_ref, ln_ref, o_ref, *, d):
    y_a = jnp.dot(oa_ref[...], wa_ref[...], preferred_element_type=F32)
    y_b = jnp.dot(ob_ref[...], wb_ref[...], preferred_element_type=F32)
    merged = jax.nn.sigmoid(g_ref[:, 0:d]) * y_a + jax.nn.sigmoid(g_ref[:, d:2 * d]) * y_b
    m = jnp.dot(merged.astype(BF16), wo_ref[...], preferred_element_type=F32)
    o_ref[...] = x_ref[...] + _rms(m, ln_ref[...])


def _post_mix(oa, ob, g, x2, wa, wb, wo, ln, *, tm):
    t, d = x2.shape
    row = lambda w: pl.BlockSpec((tm, w), lambda i: (i, 0))
    return pl.pallas_call(
        functools.partial(_post_mix_kernel, d=d),
        grid=(t // tm,),
        in_specs=[row(d), row(d), row(2 * d), row(d), _resident(wa.shape), _resident(wb.shape),
                  _resident(wo.shape), _resident(ln.shape)],
        out_specs=row(d),
        out_shape=jax.ShapeDtypeStruct((t, d), F32),
        compiler_params=pltpu.CompilerParams(
            dimension_semantics=("parallel",), vmem_limit_bytes=VMEM_LIMIT),
        name="post_mix",
    )(oa, ob, g, x2, wa, wb, wo, ln)


def _ffn_kernel(x_ref, lnp_ref, wg_ref, wu_ref, wd_ref, lno_ref, o_ref):
    x = x_ref[...]
    h = _rms(x, lnp_ref[...]).astype(BF16)
    gate = jnp.dot(h, wg_ref[...], preferred_element_type=F32)
    up = jnp.dot(h, wu_ref[...], preferred_element_type=F32)
    a = (jax.nn.silu(gate) * up).astype(BF16)
    f = jnp.dot(a, wd_ref[...], preferred_element_type=F32)
    o_ref[...] = x + _rms(f, lno_ref[...])


def _ffn(x2, lnp, wg, wu, wd, lno, *, tm):
    t, d = x2.shape
    row = pl.BlockSpec((tm, d), lambda i: (i, 0))
    return pl.pallas_call(
        _ffn_kernel,
        grid=(t // tm,),
        in_specs=[row, _resident(lnp.shape), _resident(wg.shape), _resident(wu.shape),
                  _resident(wd.shape), _resident(lno.shape)],
        out_specs=row,
        out_shape=jax.ShapeDtypeStruct((t, d), F32),
        compiler_params=pltpu.CompilerParams(
            dimension_semantics=("parallel",), vmem_limit_bytes=VMEM_LIMIT),
        name="ffn",
    )(x2, lnp, wg, wu, wd, lno)


def _pack_w_in(w, d):
    lat = MLA_Q_RANK + MLA_KV_RANK + ROPE_DIM
    zm_w = -(-lat // LANES) * LANES
    head, latent, gates = w[:, :3 * d], w[:, 3 * d:3 * d + lat], w[:, 3 * d + lat:]
    latent = jnp.pad(latent, ((0, 0), (0, zm_w - lat)))
    return jnp.concatenate([head, latent, gates], axis=1).astype(BF16), zm_w


def _pack_w_uq(w):
    r = w.shape[0]
    w = w.reshape(r, MLA_HEADS, MLA_QK_DIM)
    w = jnp.pad(w, ((0, 0), (0, 0), (0, MLA_HEAD_PAD - MLA_QK_DIM)))
    return w.reshape(r, MLA_HEADS * MLA_HEAD_PAD).astype(BF16)


def kernel(x, positions, ln_mix_pre, w_in, lambda_q1, lambda_k1, lambda_q2, lambda_k2, da_subln,
           q_a_norm, w_uq, kv_a_norm, w_ukv, w_proj_a, w_proj_b, w_o, ln_mix_post, ln_ffn_pre,
           w_ffn_gate, w_ffn_up, w_ffn_down, ln_ffn_post):
    bsz, seq, d = x.shape
    t = bsz * seq
    depth = w_in.shape[0]
    tm = 512

    half = ROPE_DIM // 2
    inv_freq = 1.0 / (ROPE_THETA ** (jnp.arange(half, dtype=F32) * (2.0 / ROPE_DIM)))
    inv_freq_lanes = jnp.tile(inv_freq, LANES // half).reshape(1, LANES)
    pos_lanes = jnp.broadcast_to(positions.reshape(t, 1), (t, LANES))
    cos_t, sin_t = _rope_tables(pos_lanes, inv_freq_lanes, 1024)

    x2 = x.reshape(t, d)
    for l in range(depth):
        lambda_init = 0.8 - 0.6 * math.exp(-0.3 * l)
        w_p, zm_w = _pack_w_in(w_in[l], d)
        q_da, k_da, v_da, zm, gates = _in_proj(
            x2, ln_mix_pre[l].reshape(1, d), w_p, cos_t, sin_t,
            bsz=bsz, seq=seq, d=d, zm_w=zm_w, g_w=2 * d, tm=tm,
            q_scale=DA_HEAD_DIM ** -0.5 * LOG2E)
        q_b, k_b, v_b = _mla_up(
            zm, q_a_norm[l].reshape(1, -1), kv_a_norm[l].reshape(1, -1),
            _pack_w_uq(w_uq[l]), w_ukv[l].astype(BF16), cos_t, sin_t,
            bsz=bsz, seq=seq, tm=tm, q_scale=MLA_QK_DIM ** -0.5 * LOG2E)

        lam_p = jnp.stack([lambda_q1[l], lambda_k1[l], lambda_q2[l], lambda_k2[l]]).astype(F32)
        lam_p = jnp.pad(lam_p, ((0, 0), (0, LANES - DA_HEAD_DIM)))
        oa = _flash(q_da, k_da, v_da, [lam_p, da_subln[l].reshape(1, -1)], blk=512, dual=True,
                    lambda_init=lambda_init)
        ob = _flash(q_b, k_b, v_b, [], blk=512, dual=False)

        x2 = _post_mix(oa.reshape(t, d), ob.reshape(t, d), gates, x2,
                       w_proj_a[l].astype(BF16), w_proj_b[l].astype(BF16), w_o[l].astype(BF16),
                       ln_mix_post[l].reshape(1, d), tm=tm)
        x2 = _ffn(x2, ln_ffn_pre[l].reshape(1, d), w_ffn_gate[l].astype(BF16),
                  w_ffn_up[l].astype(BF16), w_ffn_down[l].astype(BF16),
                  ln_ffn_post[l].reshape(1, d), tm=tm)
    return x2.reshape(bsz, seq, d)
```

```python
import functools
import math

import jax
import jax.numpy as jnp
from jax import lax
from jax.experimental import pallas as pl
from jax.experimental.pallas import tpu as pltpu

F32 = jnp.float32
BF16 = jnp.bfloat16

LANES = 128
CHUNK = 64
CHUNK_SHIFT = 6
RMS_EPS = 1e-6
ROPE_THETA = 10000.0
ROPE_DIM = 64
MASK_VALUE = -1e30
LOG2E = math.log2(math.e)

DA_HEADS = 8
DA_HEAD_DIM = 64
MLA_HEADS = 8
MLA_Q_RANK = 384
MLA_KV_RANK = 256
MLA_NOPE_DIM = 128
MLA_V_DIM = 128
MLA_QK_DIM = MLA_NOPE_DIM + ROPE_DIM
MLA_HEAD_PAD = 256

VMEM_LIMIT = 56 * 1024 * 1024


def _rms(x, w, eps=RMS_EPS):
    return x * lax.rsqrt(jnp.mean(x * x, axis=-1, keepdims=True) + eps) * w


def _rope_tile(x, cos, sin_signed, first_half):
    rot = jnp.where(first_half, pltpu.roll(x, 96, 1), pltpu.roll(x, 32, 1))
    return x * cos + rot * sin_signed


def _first_half_mask(shape):
    lane = lax.broadcasted_iota(jnp.int32, shape, 1)
    return (lane & (ROPE_DIM // 2)) == 0


def _ones_column(shape):
    lane = lax.broadcasted_iota(jnp.int32, shape, 1)
    return jnp.where(lane == 0, 1.0, 0.0).astype(BF16)


def _resident(shape):
    return pl.BlockSpec(shape, lambda *_: (0,) * len(shape), pipeline_mode=pl.Buffered(1))


def _rope_table_kernel(pos_ref, invf_ref, cos_ref, sin_ref):
    ang = pos_ref[...].astype(F32) * invf_ref[...]
    first = _first_half_mask(ang.shape)
    s = jnp.sin(ang)
    cos_ref[...] = jnp.cos(ang)
    sin_ref[...] = jnp.where(first, -s, s)


def _rope_tables(pos_lanes, inv_freq_lanes, tm):
    t = pos_lanes.shape[0]
    row = pl.BlockSpec((tm, LANES), lambda i: (i, 0))
    return pl.pallas_call(
        _rope_table_kernel,
        grid=(t // tm,),
        in_specs=[row, pl.BlockSpec((1, LANES), lambda i: (0, 0))],
        out_specs=[row, row],
        out_shape=[jax.ShapeDtypeStruct((t, LANES), F32)] * 2,
        compiler_params=pltpu.CompilerParams(dimension_semantics=("parallel",)),
        name="rope_tables",
    )(pos_lanes, inv_freq_lanes)


def _in_proj_kernel(x_ref, ln_ref, w_ref, cos_ref, sin_ref,
                    q_ref, k_ref, v_ref, zm_ref, g_ref, *, d, zm_w, g_w, nc, q_scale):
    h = _rms(x_ref[...], ln_ref[...]).astype(BF16)
    cos = cos_ref[...]
    sin = sin_ref[...]
    first = _first_half_mask(cos.shape)

    def mm(c0, width):
        return jnp.dot(h, w_ref[:, c0:c0 + width], preferred_element_type=F32)

    for c in range(0, d, nc):
        z = mm(c, nc)
        for g in range(0, nc, LANES):
            r = _rope_tile(z[:, g:g + LANES], cos, sin, first) * q_scale
            q_ref[(c + g) // LANES] = r.astype(BF16)
    for c in range(0, d, nc):
        z = mm(d + c, nc)
        for g in range(0, nc, LANES):
            r = _rope_tile(z[:, g:g + LANES], cos, sin, first)
            k_ref[(c + g) // LANES] = r.astype(BF16)
    ones_col = _ones_column((x_ref.shape[0], LANES))
    for c in range(0, d, nc):
        z = mm(2 * d + c, nc)
        for g in range(0, nc, LANES):
            v_ref[(c + g) // LANES, :, 0:LANES] = z[:, g:g + LANES].astype(BF16)
            v_ref[(c + g) // LANES, :, LANES:2 * LANES] = ones_col
    zm_ref[...] = mm(3 * d, zm_w)
    for c in range(0, g_w, nc):
        g_ref[:, c:c + nc] = mm(3 * d + zm_w + c, nc)


def _head_major_spec(heads, tm, w, seq):
    per_seq = seq // tm
    return pl.BlockSpec((None, heads, tm, w), lambda i: (i // per_seq, 0, i % per_seq, 0))


def _in_proj(x2, ln, w_p, cos_t, sin_t, *, bsz, seq, d, zm_w, g_w, tm, q_scale):
    t = x2.shape[0]
    row = lambda w: pl.BlockSpec((tm, w), lambda i: (i, 0))
    heads = d // LANES
    hm = _head_major_spec(heads, tm, LANES, seq)
    hm_v = _head_major_spec(heads, tm, 2 * LANES, seq)
    kern = functools.partial(_in_proj_kernel, d=d, zm_w=zm_w, g_w=g_w, nc=512, q_scale=q_scale)
    return pl.pallas_call(
        kern,
        grid=(t // tm,),
        in_specs=[row(d), _resident((1, d)), _resident(w_p.shape), row(LANES), row(LANES)],
        out_specs=[hm, hm, hm_v, row(zm_w), row(g_w)],
        out_shape=[jax.ShapeDtypeStruct((bsz, heads, seq, LANES), BF16)] * 2
        + [jax.ShapeDtypeStruct((bsz, heads, seq, 2 * LANES), BF16),
           jax.ShapeDtypeStruct((t, zm_w), F32), jax.ShapeDtypeStruct((t, g_w), F32)],
        compiler_params=pltpu.CompilerParams(
            dimension_semantics=("parallel",), vmem_limit_bytes=VMEM_LIMIT),
        name="in_proj",
    )(x2, ln, w_p, cos_t, sin_t)


def _mla_up_kernel(zm_ref, qn_ref, kvn_ref, wq_ref, wkv_ref, cos_ref, sin_ref,
                   q_ref, k_ref, v_ref, *, q_scale):
    cos = cos_ref[...]
    sin = sin_ref[...]
    first = _first_half_mask(cos.shape)
    cq = _rms(zm_ref[:, 0:MLA_Q_RANK], qn_ref[...]).astype(BF16)
    ckv = _rms(zm_ref[:, MLA_Q_RANK:MLA_Q_RANK + MLA_KV_RANK], kvn_ref[...]).astype(BF16)
    kr0 = MLA_Q_RANK + MLA_KV_RANK
    k_rope = _rope_tile(zm_ref[:, kr0:kr0 + LANES], cos, sin, first).astype(BF16)
    ones_col = _ones_column(cos.shape)
    for hd in range(MLA_HEADS):
        c = hd * MLA_HEAD_PAD
        qh = jnp.dot(cq, wq_ref[:, c:c + MLA_HEAD_PAD], preferred_element_type=F32)
        q_ref[hd, :, 0:LANES] = (qh[:, :LANES] * q_scale).astype(BF16)
        q_ref[hd, :, LANES:2 * LANES] = (
            _rope_tile(qh[:, LANES:], cos, sin, first) * q_scale).astype(BF16)
        kvh = jnp.dot(ckv, wkv_ref[:, c:c + MLA_HEAD_PAD], preferred_element_type=F32)
        k_ref[hd, :, 0:LANES] = kvh[:, :LANES].astype(BF16)
        k_ref[hd, :, LANES:2 * LANES] = k_rope
        v_ref[hd, :, 0:LANES] = kvh[:, LANES:].astype(BF16)
        v_ref[hd, :, LANES:2 * LANES] = ones_col


def _mla_up(zm, qn, kvn, wq_p, wkv, cos_t, sin_t, *, bsz, seq, tm, q_scale):
    t, zm_w = zm.shape
    row = lambda w: pl.BlockSpec((tm, w), lambda i: (i, 0))
    qk = _head_major_spec(MLA_HEADS, tm, MLA_HEAD_PAD, seq)
    vv = _head_major_spec(MLA_HEADS, tm, MLA_V_DIM + LANES, seq)
    return pl.pallas_call(
        functools.partial(_mla_up_kernel, q_scale=q_scale),
        grid=(t // tm,),
        in_specs=[row(zm_w), _resident(qn.shape), _resident(kvn.shape),
                  _resident(wq_p.shape), _resident(wkv.shape), row(LANES), row(LANES)],
        out_specs=[qk, qk, vv],
        out_shape=[jax.ShapeDtypeStruct((bsz, MLA_HEADS, seq, MLA_HEAD_PAD), BF16)] * 2
        + [jax.ShapeDtypeStruct((bsz, MLA_HEADS, seq, MLA_V_DIM + LANES), BF16)],
        compiler_params=pltpu.CompilerParams(
            dimension_semantics=("parallel",), vmem_limit_bytes=VMEM_LIMIT),
        name="mla_up",
    )(zm, qn, kvn, wq_p, wkv, cos_t, sin_t)


def _flash_kernel(q_ref, k_ref, v_ref, *rest, tq, dual, lambda_init):
    if dual:
        lam_ref, subln_ref, o_ref, qz_sc, sa_sc, sb_sc, m_sc, acc_sc = rest
    else:
        o_ref, sa_sc, sb_sc, m_sc, acc_sc = rest
    i = pl.program_id(2)
    dv = o_ref.shape[-1]

    if dual:
        q_in = q_ref[...]
        lo = lax.broadcasted_iota(jnp.int32, q_in.shape, 1) < DA_HEAD_DIM
        zero = jnp.zeros_like(q_in)
        qz_sc[0:tq, :] = jnp.where(lo, q_in, zero)
        qz_sc[tq:, :] = jnp.where(lo, zero, q_in)
    m_sc[...] = jnp.full(m_sc.shape, -jnp.inf, F32)
    acc_sc[...] = jnp.zeros(acc_sc.shape, F32)

    def keys(ref, j):
        return ref[pl.ds(pl.multiple_of(j * tq, tq), tq), :]

    def qk(s_ref, j):
        q = qz_sc[...] if dual else q_ref[...]
        s_ref[...] = lax.dot_general(q, keys(k_ref, j), (((1,), (1,)), ((), ())),
                                     preferred_element_type=F32)

    def softmax_pv(s_ref, j, masked):
        s = s_ref[...]
        if masked:
            r = lax.broadcasted_iota(jnp.int32, s.shape, 0) & (tq - 1)
            c = lax.broadcasted_iota(jnp.int32, s.shape, 1)
            s = jnp.where((c >> CHUNK_SHIFT) <= (r >> CHUNK_SHIFT), s, MASK_VALUE)
        m_prev = m_sc[...]
        m_new = jnp.maximum(m_prev, jnp.max(s, axis=1, keepdims=True))
        alpha = jnp.exp2(m_prev - m_new)
        p = jnp.exp2((s - jnp.concatenate([m_new] * (tq // LANES), axis=1)).astype(BF16))
        pv = jnp.dot(p, keys(v_ref, j), preferred_element_type=F32)
        acc_sc[...] = jnp.concatenate([alpha] * (acc_sc.shape[1] // LANES), axis=1) * acc_sc[...] + pv
        m_sc[...] = m_new

    qk(sa_sc, 0)

    def pair(t, carry):
        qk(sb_sc, 2 * t + 1)
        softmax_pv(sa_sc, 2 * t, False)
        qk(sa_sc, 2 * t + 2)
        softmax_pv(sb_sc, 2 * t + 1, False)
        return carry

    lax.fori_loop(0, i // 2, pair, 0)

    @pl.when(i % 2 == 0)
    def _even_tail():
        softmax_pv(sa_sc, i, True)

    @pl.when(i % 2 == 1)
    def _odd_tail():
        qk(sb_sc, i)
        softmax_pv(sa_sc, i - 1, False)
        softmax_pv(sb_sc, i, True)

    o = acc_sc[:, 0:dv] / acc_sc[:, dv:dv + 1]
    if dual:
        lp = lam_ref[...]
        lam = (jnp.exp(jnp.sum(lp[0:1] * lp[1:2], axis=1, keepdims=True))
               - jnp.exp(jnp.sum(lp[2:3] * lp[3:4], axis=1, keepdims=True))
               + lambda_init)
        oa = o[0:tq] - lam * o[tq:]
        o_ref[...] = (_rms(oa, subln_ref[...]) * (1.0 - lambda_init)).astype(BF16)
    else:
        o_ref[...] = o.astype(BF16)


def _flash(q, k, v, extras, *, blk, dual, lambda_init=0.0):
    bsz, heads, seq, dk = q.shape
    dvp = v.shape[-1]
    dv = dvp - LANES
    assert blk % CHUNK == 0 and blk & (blk - 1) == 0 and seq % blk == 0
    rows = 2 * blk if dual else blk
    qspec = pl.BlockSpec((None, None, blk, dk), lambda b, h, i: (b, h, i, 0))
    kspec = pl.BlockSpec((None, None, seq, dk), lambda b, h, i: (b, h, 0, 0))
    vspec = pl.BlockSpec((None, None, seq, dvp), lambda b, h, i: (b, h, 0, 0))
    ospec = pl.BlockSpec((None, blk, dv), lambda b, h, i: (b, i, h))
    espec = [pl.BlockSpec(e.shape, lambda b, h, i: (0, 0)) for e in extras]
    scratch = [pltpu.VMEM((rows, blk), F32), pltpu.VMEM((rows, blk), F32),
               pltpu.VMEM((rows, LANES), F32), pltpu.VMEM((rows, dvp), F32)]
    if dual:
        scratch = [pltpu.VMEM((rows, dk), BF16)] + scratch
    kern = functools.partial(_flash_kernel, tq=blk, dual=dual, lambda_init=lambda_init)
    return pl.pallas_call(
        kern,
        grid=(bsz, heads, seq // blk),
        in_specs=[qspec, kspec, vspec] + espec,
        out_specs=ospec,
        scratch_shapes=scratch,
        out_shape=jax.ShapeDtypeStruct((bsz, seq, heads * dv), BF16),
        compiler_params=pltpu.CompilerParams(
            dimension_semantics=("parallel", "parallel", "arbitrary"),
            vmem_limit_bytes=VMEM_LIMIT),
        name="flash_da" if dual else "flash_mla",
    )(q, k, v, *extras)


def _post_mix_kernel(oa_ref, ob_ref, g_ref, x_ref, wa_ref, wb_ref, wo_ref, ln_ref, o_ref, *, d):
    y_a = jnp.dot(oa_ref[...], wa_ref[...], preferred_element_type=F32)
    y_b = jnp.dot(ob_ref[...], wb_ref[...], preferred_element_type=F32)
    merged = jax.nn.sigmoid(g_ref[:, 0:d]) * y_a + jax.nn.sigmoid(g_ref[:, d:2 * d]) * y_b
    m = jnp.dot(merged.astype(BF16), wo_ref[...], preferred_element_type=F32)
    o_ref[...] = x_ref[...] + _rms(m, ln_ref[...])


def _post_mix(oa, ob, g, x2, wa, wb, wo, ln, *, tm):
    t, d = x2.shape
    row = lambda w: pl.BlockSpec((tm, w), lambda i: (i, 0))
    return pl.pallas_call(
        functools.partial(_post_mix_kernel, d=d),
        grid=(t // tm,),
        in_specs=[row(d), row(d), row(2 * d), row(d), _resident(wa.shape), _resident(wb.shape),
                  _resident(wo.shape), _resident(ln.shape)],
        out_specs=row(d),
        out_shape=jax.ShapeDtypeStruct((t, d), F32),
        compiler_params=pltpu.CompilerParams(
            dimension_semantics=("parallel",), vmem_limit_bytes=VMEM_LIMIT),
        name="post_mix",
    )(oa, ob, g, x2, wa, wb, wo, ln)


def _ffn_kernel(x_ref, lnp_ref, wg_ref, wu_ref, wd_ref, lno_ref, o_ref):
    x = x_ref[...]
    h = _rms(x, lnp_ref[...]).astype(BF16)
    gate = jnp.dot(h, wg_ref[...], preferred_element_type=F32)
    up = jnp.dot(h, wu_ref[...], preferred_element_type=F32)
    a = (jax.nn.silu(gate) * up).astype(BF16)
    f = jnp.dot(a, wd_ref[...], preferred_element_type=F32)
    o_ref[...] = x + _rms(f, lno_ref[...])


def _ffn(x2, lnp, wg, wu, wd, lno, *, tm):
    t, d = x2.shape
    row = pl.BlockSpec((tm, d), lambda i: (i, 0))
    return pl.pallas_call(
        _ffn_kernel,
        grid=(t // tm,),
        in_specs=[row, _resident(lnp.shape), _resident(wg.shape), _resident(wu.shape),
                  _resident(wd.shape), _resident(lno.shape)],
        out_specs=row,
        out_shape=jax.ShapeDtypeStruct((t, d), F32),
        compiler_params=pltpu.CompilerParams(
            dimension_semantics=("parallel",), vmem_limit_bytes=VMEM_LIMIT),
        name="ffn",
    )(x2, lnp, wg, wu, wd, lno)


def _pack_w_in(w, d):
    lat = MLA_Q_RANK + MLA_KV_RANK + ROPE_DIM
    zm_w = -(-lat // LANES) * LANES
    head, latent, gates = w[:, :3 * d], w[:, 3 * d:3 * d + lat], w[:, 3 * d + lat:]
    latent = jnp.pad(latent, ((0, 0), (0, zm_w - lat)))
    return jnp.concatenate([head, latent, gates], axis=1).astype(BF16), zm_w


def _pack_w_uq(w):
    r = w.shape[0]
    w = w.reshape(r, MLA_HEADS, MLA_QK_DIM)
    w = jnp.pad(w, ((0, 0), (0, 0), (0, MLA_HEAD_PAD - MLA_QK_DIM)))
    return w.reshape(r, MLA_HEADS * MLA_HEAD_PAD).astype(BF16)


def kernel(x, positions, ln_mix_pre, w_in, lambda_q1, lambda_k1, lambda_q2, lambda_k2, da_subln,
           q_a_norm, w_uq, kv_a_norm, w_ukv, w_proj_a, w_proj_b, w_o, ln_mix_post, ln_ffn_pre,
           w_ffn_gate, w_ffn_up, w_ffn_down, ln_ffn_post):
    bsz, seq, d = x.shape
    t = bsz * seq
    depth = w_in.shape[0]
    tm = 512

    half = ROPE_DIM // 2
    inv_freq = 1.0 / (ROPE_THETA ** (jnp.arange(half, dtype=F32) * (2.0 / ROPE_DIM)))
    inv_freq_lanes = jnp.tile(inv_freq, LANES // half).reshape(1, LANES)
    pos_lanes = jnp.broadcast_to(positions.reshape(t, 1), (t, LANES))
    cos_t, sin_t = _rope_tables(pos_lanes, inv_freq_lanes, 1024)

    x2 = x.reshape(t, d)
    for l in range(depth):
        lambda_init = 0.8 - 0.6 * math.exp(-0.3 * l)
        w_p, zm_w = _pack_w_in(w_in[l], d)
        q_da, k_da, v_da, zm, gates = _in_proj(
            x2, ln_mix_pre[l].reshape(1, d), w_p, cos_t, sin_t,
            bsz=bsz, seq=seq, d=d, zm_w=zm_w, g_w=2 * d, tm=tm,
            q_scale=DA_HEAD_DIM ** -0.5 * LOG2E)
        q_b, k_b, v_b = _mla_up(
            zm, q_a_norm[l].reshape(1, -1), kv_a_norm[l].reshape(1, -1),
            _pack_w_uq(w_uq[l]), w_ukv[l].astype(BF16), cos_t, sin_t,
            bsz=bsz, seq=seq, tm=tm, q_scale=MLA_QK_DIM ** -0.5 * LOG2E)

        lam_p = jnp.stack([lambda_q1[l], lambda_k1[l], lambda_q2[l], lambda_k2[l]]).astype(F32)
        lam_p = jnp.pad(lam_p, ((0, 0), (0, LANES - DA_HEAD_DIM)))
        oa = _flash(q_da, k_da, v_da, [lam_p, da_subln[l].reshape(1, -1)], blk=512, dual=True,
                    lambda_init=lambda_init)
        ob = _flash(q_b, k_b, v_b, [], blk=1024, dual=False)

        x2 = _post_mix(oa.reshape(t, d), ob.reshape(t, d), gates, x2,
                       w_proj_a[l].astype(BF16), w_proj_b[l].astype(BF16), w_o[l].astype(BF16),
                       ln_mix_post[l].reshape(1, d), tm=tm)
        x2 = _ffn(x2, ln_ffn_pre[l].reshape(1, d), w_ffn_gate[l].astype(BF16),
                  w_ffn_up[l].astype(BF16), w_ffn_down[l].astype(BF16),
                  ln_ffn_post[l].reshape(1, d), tm=tm)
    return x2.reshape(bsz, seq, d)
```

```python
import functools
import math

import jax
import jax.numpy as jnp
from jax import lax
from jax.experimental import pallas as pl
from jax.experimental.pallas import tpu as pltpu

F32 = jnp.float32
BF16 = jnp.bfloat16

LANES = 128
CHUNK = 64
CHUNK_SHIFT = 6
RMS_EPS = 1e-6
ROPE_THETA = 10000.0
ROPE_DIM = 64
MASK_VALUE = -1e30
LOG2E = math.log2(math.e)

DA_HEADS = 8
DA_HEAD_DIM = 64
MLA_HEADS = 8
MLA_Q_RANK = 384
MLA_KV_RANK = 256
MLA_NOPE_DIM = 128
MLA_V_DIM = 128
MLA_QK_DIM = MLA_NOPE_DIM + ROPE_DIM
MLA_HEAD_PAD = 256

VMEM_LIMIT = 56 * 1024 * 1024


def _rms(x, w, eps=RMS_EPS):
    return x * lax.rsqrt(jnp.mean(x * x, axis=-1, keepdims=True) + eps) * w


def _rope_tile(x, cos, sin_signed, first_half):
    rot = jnp.where(first_half, pltpu.roll(x, 96, 1), pltpu.roll(x, 32, 1))
    return x * cos + rot * sin_signed


def _first_half_mask(shape):
    lane = lax.broadcasted_iota(jnp.int32, shape, 1)
    return (lane & (ROPE_DIM // 2)) == 0


def _ones_column(shape):
    lane = lax.broadcasted_iota(jnp.int32, shape, 1)
    return jnp.where(lane == 0, 1.0, 0.0).astype(BF16)


def _resident(shape):
    return pl.BlockSpec(shape, lambda *_: (0,) * len(shape), pipeline_mode=pl.Buffered(1))


def _rope_table_kernel(pos_ref, invf_ref, cos_ref, sin_ref):
    ang = pos_ref[...].astype(F32) * invf_ref[...]
    first = _first_half_mask(ang.shape)
    s = jnp.sin(ang)
    cos_ref[...] = jnp.cos(ang)
    sin_ref[...] = jnp.where(first, -s, s)


def _rope_tables(pos_lanes, inv_freq_lanes, tm):
    t = pos_lanes.shape[0]
    row = pl.BlockSpec((tm, LANES), lambda i: (i, 0))
    return pl.pallas_call(
        _rope_table_kernel,
        grid=(t // tm,),
        in_specs=[row, pl.BlockSpec((1, LANES), lambda i: (0, 0))],
        out_specs=[row, row],
        out_shape=[jax.ShapeDtypeStruct((t, LANES), F32)] * 2,
        compiler_params=pltpu.CompilerParams(dimension_semantics=("parallel",)),
        name="rope_tables",
    )(pos_lanes, inv_freq_lanes)


def _in_proj_kernel(x_ref, ln_ref, w_ref, cos_ref, sin_ref,
                    q_ref, k_ref, v_ref, zm_ref, g_ref, *, d, zm_w, g_w, nc, q_scale):
    h = _rms(x_ref[...], ln_ref[...]).astype(BF16)
    cos = cos_ref[...]
    sin = sin_ref[...]
    first = _first_half_mask(cos.shape)

    def mm(c0, width):
        return jnp.dot(h, w_ref[:, c0:c0 + width], preferred_element_type=F32)

    for c in range(0, d, nc):
        z = mm(c, nc)
        for g in range(0, nc, LANES):
            r = _rope_tile(z[:, g:g + LANES], cos, sin, first) * q_scale
            q_ref[(c + g) // LANES] = r.astype(BF16)
    for c in range(0, d, nc):
        z = mm(d + c, nc)
        for g in range(0, nc, LANES):
            r = _rope_tile(z[:, g:g + LANES], cos, sin, first)
            k_ref[(c + g) // LANES] = r.astype(BF16)
    ones_col = _ones_column((x_ref.shape[0], LANES))
    for c in range(0, d, nc):
        z = mm(2 * d + c, nc)
        for g in range(0, nc, LANES):
            v_ref[(c + g) // LANES, :, 0:LANES] = z[:, g:g + LANES].astype(BF16)
            v_ref[(c + g) // LANES, :, LANES:2 * LANES] = ones_col
    zm_ref[...] = mm(3 * d, zm_w)
    for c in range(0, g_w, nc):
        g_ref[:, c:c + nc] = mm(3 * d + zm_w + c, nc)


def _head_major_spec(heads, tm, w, seq):
    per_seq = seq // tm
    return pl.BlockSpec((None, heads, tm, w), lambda i: (i // per_seq, 0, i % per_seq, 0))


def _in_proj(x2, ln, w_p, cos_t, sin_t, *, bsz, seq, d, zm_w, g_w, tm, q_scale):
    t = x2.shape[0]
    row = lambda w: pl.BlockSpec((tm, w), lambda i: (i, 0))
    heads = d // LANES
    hm = _head_major_spec(heads, tm, LANES, seq)
    hm_v = _head_major_spec(heads, tm, 2 * LANES, seq)
    kern = functools.partial(_in_proj_kernel, d=d, zm_w=zm_w, g_w=g_w, nc=512, q_scale=q_scale)
    return pl.pallas_call(
        kern,
        grid=(t // tm,),
        in_specs=[row(d), _resident((1, d)), _resident(w_p.shape), row(LANES), row(LANES)],
        out_specs=[hm, hm, hm_v, row(zm_w), row(g_w)],
        out_shape=[jax.ShapeDtypeStruct((bsz, heads, seq, LANES), BF16)] * 2
        + [jax.ShapeDtypeStruct((bsz, heads, seq, 2 * LANES), BF16),
           jax.ShapeDtypeStruct((t, zm_w), F32), jax.ShapeDtypeStruct((t, g_w), F32)],
        compiler_params=pltpu.CompilerParams(
            dimension_semantics=("parallel",), vmem_limit_bytes=VMEM_LIMIT),
        name="in_proj",
    )(x2, ln, w_p, cos_t, sin_t)


def _mla_up_kernel(zm_ref, qn_ref, kvn_ref, wq_ref, wkv_ref, cos_ref, sin_ref,
                   q_ref, k_ref, v_ref, *, q_scale):
    cos = cos_ref[...]
    sin = sin_ref[...]
    first = _first_half_mask(cos.shape)
    cq = _rms(zm_ref[:, 0:MLA_Q_RANK], qn_ref[...]).astype(BF16)
    ckv = _rms(zm_ref[:, MLA_Q_RANK:MLA_Q_RANK + MLA_KV_RANK], kvn_ref[...]).astype(BF16)
    kr0 = MLA_Q_RANK + MLA_KV_RANK
    k_rope = _rope_tile(zm_ref[:, kr0:kr0 + LANES], cos, sin, first).astype(BF16)
    ones_col = _ones_column(cos.shape)
    for hd in range(MLA_HEADS):
        c = hd * MLA_HEAD_PAD
        qh = jnp.dot(cq, wq_ref[:, c:c + MLA_HEAD_PAD], preferred_element_type=F32)
        q_ref[hd, :, 0:LANES] = (qh[:, :LANES] * q_scale).astype(BF16)
        q_ref[hd, :, LANES:2 * LANES] = (
            _rope_tile(qh[:, LANES:], cos, sin, first) * q_scale).astype(BF16)
        kvh = jnp.dot(ckv, wkv_ref[:, c:c + MLA_HEAD_PAD], preferred_element_type=F32)
        k_ref[hd, :, 0:LANES] = kvh[:, :LANES].astype(BF16)
        k_ref[hd, :, LANES:2 * LANES] = k_rope
        v_ref[hd, :, 0:LANES] = kvh[:, LANES:].astype(BF16)
        v_ref[hd, :, LANES:2 * LANES] = ones_col


def _mla_up(zm, qn, kvn, wq_p, wkv, cos_t, sin_t, *, bsz, seq, tm, q_scale):
    t, zm_w = zm.shape
    row = lambda w: pl.BlockSpec((tm, w), lambda i: (i, 0))
    qk = _head_major_spec(MLA_HEADS, tm, MLA_HEAD_PAD, seq)
    vv = _head_major_spec(MLA_HEADS, tm, MLA_V_DIM + LANES, seq)
    return pl.pallas_call(
        functools.partial(_mla_up_kernel, q_scale=q_scale),
        grid=(t // tm,),
        in_specs=[row(zm_w), _resident(qn.shape), _resident(kvn.shape),
                  _resident(wq_p.shape), _resident(wkv.shape), row(LANES), row(LANES)],
        out_specs=[qk, qk, vv],
        out_shape=[jax.ShapeDtypeStruct((bsz, MLA_HEADS, seq, MLA_HEAD_PAD), BF16)] * 2
        + [jax.ShapeDtypeStruct((bsz, MLA_HEADS, seq, MLA_V_DIM + LANES), BF16)],
        compiler_params=pltpu.CompilerParams(
            dimension_semantics=("parallel",), vmem_limit_bytes=VMEM_LIMIT),
        name="mla_up",
    )(zm, qn, kvn, wq_p, wkv, cos_t, sin_t)


def _flash_kernel(q_ref, k_ref, v_ref, *rest, tq, dual, lambda_init):
    if dual:
        lam_ref, subln_ref, o_ref, qz_sc, sa_sc, sb_sc, m_sc, acc_sc = rest
    else:
        o_ref, sa_sc, sb_sc, m_sc, acc_sc = rest
    i = pl.program_id(2)
    dv = o_ref.shape[-1]

    if dual:
        q_in = q_ref[...]
        lo = lax.broadcasted_iota(jnp.int32, q_in.shape, 1) < DA_HEAD_DIM
        zero = jnp.zeros_like(q_in)
        qz_sc[0:tq, :] = jnp.where(lo, q_in, zero)
        qz_sc[tq:, :] = jnp.where(lo, zero, q_in)
    m_sc[...] = jnp.full(m_sc.shape, -jnp.inf, F32)
    acc_sc[...] = jnp.zeros(acc_sc.shape, F32)

    def keys(ref, j):
        return ref[pl.ds(pl.multiple_of(j * tq, tq), tq), :]

    def qk(s_ref, j):
        q = qz_sc[...] if dual else q_ref[...]
        s_ref[...] = lax.dot_general(q, keys(k_ref, j), (((1,), (1,)), ((), ())),
                                     preferred_element_type=F32)

    def softmax_pv(s_ref, j, masked):
        s = s_ref[...]
        if masked:
            r = lax.broadcasted_iota(jnp.int32, s.shape, 0) & (tq - 1)
            c = lax.broadcasted_iota(jnp.int32, s.shape, 1)
            s = jnp.where((c >> CHUNK_SHIFT) <= (r >> CHUNK_SHIFT), s, MASK_VALUE)
        m_prev = m_sc[...]
        m_new = jnp.maximum(m_prev, jnp.max(s, axis=1, keepdims=True))
        alpha = jnp.exp2(m_prev - m_new)
        p = jnp.exp2((s - jnp.concatenate([m_new] * (tq // LANES), axis=1)).astype(BF16))
        pv = jnp.dot(p, keys(v_ref, j), preferred_element_type=F32)
        acc_sc[...] = jnp.concatenate([alpha] * (acc_sc.shape[1] // LANES), axis=1) * acc_sc[...] + pv
        m_sc[...] = m_new

    qk(sa_sc, 0)

    def pair(t, carry):
        qk(sb_sc, 2 * t + 1)
        softmax_pv(sa_sc, 2 * t, False)
        qk(sa_sc, 2 * t + 2)
        softmax_pv(sb_sc, 2 * t + 1, False)
        return carry

    lax.fori_loop(0, i // 2, pair, 0)

    @pl.when(i % 2 == 0)
    def _even_tail():
        softmax_pv(sa_sc, i, True)

    @pl.when(i % 2 == 1)
    def _odd_tail():
        qk(sb_sc, i)
        softmax_pv(sa_sc, i - 1, False)
        softmax_pv(sb_sc, i, True)

    o = acc_sc[:, 0:dv] / acc_sc[:, dv:dv + 1]
    if dual:
        lp = lam_ref[...]
        lam = (jnp.exp(jnp.sum(lp[0:1] * lp[1:2], axis=1, keepdims=True))
               - jnp.exp(jnp.sum(lp[2:3] * lp[3:4], axis=1, keepdims=True))
               + lambda_init)
        oa = o[0:tq] - lam * o[tq:]
        o_ref[...] = (_rms(oa, subln_ref[...]) * (1.0 - lambda_init)).astype(BF16)
    else:
        o_ref[...] = o.astype(BF16)


def _flash(q, k, v, extras, *, blk, dual, lambda_init=0.0):
    bsz, heads, seq, dk = q.shape
    dvp = v.shape[-1]
    dv = dvp - LANES
    assert blk % CHUNK == 0 and blk & (blk - 1) == 0 and seq % blk == 0
    rows = 2 * blk if dual else blk
    qspec = pl.BlockSpec((None, None, blk, dk), lambda b, h, i: (b, h, i, 0))
    kspec = pl.BlockSpec((None, None, seq, dk), lambda b, h, i: (b, h, 0, 0))
    vspec = pl.BlockSpec((None, None, seq, dvp), lambda b, h, i: (b, h, 0, 0))
    ospec = pl.BlockSpec((None, blk, dv), lambda b, h, i: (b, i, h))
    espec = [pl.BlockSpec(e.shape, lambda b, h, i: (0, 0)) for e in extras]
    scratch = [pltpu.VMEM((rows, blk), F32), pltpu.VMEM((rows, blk), F32),
               pltpu.VMEM((rows, LANES), F32), pltpu.VMEM((rows, dvp), F32)]
    if dual:
        scratch = [pltpu.VMEM((rows, dk), BF16)] + scratch
    kern = functools.partial(_flash_kernel, tq=blk, dual=dual, lambda_init=lambda_init)
    return pl.pallas_call(
        kern,
        grid=(bsz, heads, seq // blk),
        in_specs=[qspec, kspec, vspec] + espec,
        out_specs=ospec,
        scratch_shapes=scratch,
        out_shape=jax.ShapeDtypeStruct((bsz, seq, heads * dv), BF16),
        compiler_params=pltpu.CompilerParams(
            dimension_semantics=("parallel", "parallel", "arbitrary"),
            vmem_limit_bytes=VMEM_LIMIT),
        name="flash_da" if dual else "flash_mla",
    )(q, k, v, *extras)


def _post_mix_kernel(oa_ref, ob_ref, g_ref, x_ref, wa_ref, wb_ref, wo_ref, ln_ref, o_ref, *, d):
    y_a = jnp.dot(oa_ref[...], wa_ref[...], preferred_element_type=F32)
    y_b = jnp.dot(ob_ref[...], wb_ref[...], preferred_element_type=F32)
    merged = jax.nn.sigmoid(g_ref[:, 0:d]) * y_a + jax.nn.sigmoid(g_ref[:, d:2 * d]) * y_b
    m = jnp.dot(merged.astype(BF16), wo_ref[...], preferred_element_type=F32)
    o_ref[...] = x_ref[...] + _rms(m, ln_ref[...])


def _post_mix(oa, ob, g, x2, wa, wb, wo, ln, *, tm):
    t, d = x2.shape
    row = lambda w: pl.BlockSpec((tm, w), lambda i: (i, 0))
    return pl.pallas_call(
        functools.partial(_post_mix_kernel, d=d),
        grid=(t // tm,),
        in_specs=[row(d), row(d), row(2 * d), row(d), _resident(wa.shape), _resident(wb.shape),
                  _resident(wo.shape), _resident(ln.shape)],
        out_specs=row(d),
        out_shape=jax.ShapeDtypeStruct((t, d), F32),
        compiler_params=pltpu.CompilerParams(
            dimension_semantics=("parallel",), vmem_limit_bytes=VMEM_LIMIT),
        name="post_mix",
    )(oa, ob, g, x2, wa, wb, wo, ln)


def _ffn_kernel(x_ref, lnp_ref, wg_ref, wu_ref, wd_ref, lno_ref, o_ref):
    x = x_ref[...]
    h = _rms(x, lnp_ref[...]).astype(BF16)
    gate = jnp.dot(h, wg_ref[...], preferred_element_type=F32)
    up = jnp.dot(h, wu_ref[...], preferred_element_type=F32)
    a = (jax.nn.silu(gate) * up).astype(BF16)
    f = jnp.dot(a, wd_ref[...], preferred_element_type=F32)
    o_ref[...] = x + _rms(f, lno_ref[...])


def _ffn(x2, lnp, wg, wu, wd, lno, *, tm):
    t, d = x2.shape
    row = pl.BlockSpec((tm, d), lambda i: (i, 0))
    return pl.pallas_call(
        _ffn_kernel,
        grid=(t // tm,),
        in_specs=[row, _resident(lnp.shape), _resident(wg.shape), _resident(wu.shape),
                  _resident(wd.shape), _resident(lno.shape)],
        out_specs=row,
        out_shape=jax.ShapeDtypeStruct((t, d), F32),
        compiler_params=pltpu.CompilerParams(
            dimension_semantics=("parallel",), vmem_limit_bytes=VMEM_LIMIT),
        name="ffn",
    )(x2, lnp, wg, wu, wd, lno)


def _pack_w_in(w, d):
    lat = MLA_Q_RANK + MLA_KV_RANK + ROPE_DIM
    zm_w = -(-lat // LANES) * LANES
    head, latent, gates = w[:, :3 * d], w[:, 3 * d:3 * d + lat], w[:, 3 * d + lat:]
    latent = jnp.pad(latent, ((0, 0), (0, zm_w - lat)))
    return jnp.concatenate([head, latent, gates], axis=1).astype(BF16), zm_w


def _pack_w_uq(w):
    r = w.shape[0]
    w = w.reshape(r, MLA_HEADS, MLA_QK_DIM)
    w = jnp.pad(w, ((0, 0), (0, 0), (0, MLA_HEAD_PAD - MLA_QK_DIM)))
    return w.reshape(r, MLA_HEADS * MLA_HEAD_PAD).astype(BF16)


def kernel(x, positions, ln_mix_pre, w_in, lambda_q1, lambda_k1, lambda_q2, lambda_k2, da_subln,
           q_a_norm, w_uq, kv_a_norm, w_ukv, w_proj_a, w_proj_b, w_o, ln_mix_post, ln_ffn_pre,
           w_ffn_gate, w_ffn_up, w_ffn_down, ln_ffn_post):
    bsz, seq, d = x.shape
    t = bsz * seq
    depth = w_in.shape[0]
    tm = 512

    half = ROPE_DIM // 2
    inv_freq = 1.0 / (ROPE_THETA ** (jnp.arange(half, dtype=F32) * (2.0 / ROPE_DIM)))
    inv_freq_lanes = jnp.tile(inv_freq, LANES // half).reshape(1, LANES)
    pos_lanes = jnp.broadcast_to(positions.reshape(t, 1), (t, LANES))
    cos_t, sin_t = _rope_tables(pos_lanes, inv_freq_lanes, 1024)

    x2 = x.reshape(t, d)
    for l in range(depth):
        lambda_init = 0.8 - 0.6 * math.exp(-0.3 * l)
        w_p, zm_w = _pack_w_in(w_in[l], d)
        q_da, k_da, v_da, zm, gates = _in_proj(
            x2, ln_mix_pre[l].reshape(1, d), w_p, cos_t, sin_t,
            bsz=bsz, seq=seq, d=d, zm_w=zm_w, g_w=2 * d, tm=tm,
            q_scale=DA_HEAD_DIM ** -0.5 * LOG2E)
        q_b, k_b, v_b = _mla_up(
            zm, q_a_norm[l].reshape(1, -1), kv_a_norm[l].reshape(1, -1),
            _pack_w_uq(w_uq[l]), w_ukv[l].astype(BF16), cos_t, sin_t,
            bsz=bsz, seq=seq, tm=tm, q_scale=MLA_QK_DIM ** -0.5 * LOG2E)

        lam_p = jnp.stack([lambda_q1[l], lambda_k1[l], lambda_q2[l], lambda_k2[l]]).astype(F32)
        lam_p = jnp.pad(lam_p, ((0, 0), (0, LANES - DA_HEAD_DIM)))
        oa = _flash(q_da, k_da, v_da, [lam_p, da_subln[l].reshape(1, -1)], blk=1024, dual=True,
                    lambda_init=lambda_init)
        ob = _flash(q_b, k_b, v_b, [], blk=1024, dual=False)

        x2 = _post_mix(oa.reshape(t, d), ob.reshape(t, d), gates, x2,
                       w_proj_a[l].astype(BF16), w_proj_b[l].astype(BF16), w_o[l].astype(BF16),
                       ln_mix_post[l].reshape(1, d), tm=tm)
        x2 = _ffn(x2, ln_ffn_pre[l].reshape(1, d), w_ffn_gate[l].astype(BF16),
                  w_ffn_up[l].astype(BF16), w_ffn_down[l].astype(BF16),
                  ln_ffn_post[l].reshape(1, d), tm=tm)
    return x2.reshape(bsz, seq, d)
```

```python
import functools
import math

import jax
import jax.numpy as jnp
from jax import lax
from jax.experimental import pallas as pl
from jax.experimental.pallas import tpu as pltpu

F32 = jnp.float32
BF16 = jnp.bfloat16

LANES = 128
CHUNK = 64
CHUNK_SHIFT = 6
RMS_EPS = 1e-6
ROPE_THETA = 10000.0
ROPE_DIM = 64
MASK_VALUE = -1e30
LOG2E = math.log2(math.e)

DA_HEADS = 8
DA_HEAD_DIM = 64
MLA_HEADS = 8
MLA_Q_RANK = 384
MLA_KV_RANK = 256
MLA_NOPE_DIM = 128
MLA_V_DIM = 128
MLA_QK_DIM = MLA_NOPE_DIM + ROPE_DIM
MLA_HEAD_PAD = 256

VMEM_LIMIT = 56 * 1024 * 1024


def _rms(x, w, eps=RMS_EPS):
    return x * lax.rsqrt(jnp.mean(x * x, axis=-1, keepdims=True) + eps) * w


def _rope_tile(x, cos, sin_signed, first_half):
    rot = jnp.where(first_half, pltpu.roll(x, 96, 1), pltpu.roll(x, 32, 1))
    return x * cos + rot * sin_signed


def _first_half_mask(shape):
    lane = lax.broadcasted_iota(jnp.int32, shape, 1)
    return (lane & (ROPE_DIM // 2)) == 0


def _ones_column(shape):
    lane = lax.broadcasted_iota(jnp.int32, shape, 1)
    return jnp.where(lane == 0, 1.0, 0.0).astype(BF16)


def _resident(shape):
    return pl.BlockSpec(shape, lambda *_: (0,) * len(shape), pipeline_mode=pl.Buffered(1))


def _rope_table_kernel(pos_ref, invf_ref, cos_ref, sin_ref):
    ang = pos_ref[...].astype(F32) * invf_ref[...]
    first = _first_half_mask(ang.shape)
    s = jnp.sin(ang)
    cos_ref[...] = jnp.cos(ang)
    sin_ref[...] = jnp.where(first, -s, s)


def _rope_tables(pos_lanes, inv_freq_lanes, tm):
    t = pos_lanes.shape[0]
    row = pl.BlockSpec((tm, LANES), lambda i: (i, 0))
    return pl.pallas_call(
        _rope_table_kernel,
        grid=(t // tm,),
        in_specs=[row, pl.BlockSpec((1, LANES), lambda i: (0, 0))],
        out_specs=[row, row],
        out_shape=[jax.ShapeDtypeStruct((t, LANES), F32)] * 2,
        compiler_params=pltpu.CompilerParams(dimension_semantics=("parallel",)),
        name="rope_tables",
    )(pos_lanes, inv_freq_lanes)


def _in_proj_kernel(x_ref, ln_ref, w_ref, cos_ref, sin_ref,
                    q_ref, k_ref, v_ref, zm_ref, g_ref, *, d, zm_w, g_w, nc, q_scale):
    h = _rms(x_ref[...], ln_ref[...]).astype(BF16)
    cos = cos_ref[...]
    sin = sin_ref[...]
    first = _first_half_mask(cos.shape)

    def mm(c0, width):
        return jnp.dot(h, w_ref[:, c0:c0 + width], preferred_element_type=F32)

    for c in range(0, d, nc):
        z = mm(c, nc)
        for g in range(0, nc, LANES):
            r = _rope_tile(z[:, g:g + LANES], cos, sin, first) * q_scale
            q_ref[(c + g) // LANES] = r.astype(BF16)
    for c in range(0, d, nc):
        z = mm(d + c, nc)
        for g in range(0, nc, LANES):
            r = _rope_tile(z[:, g:g + LANES], cos, sin, first)
            k_ref[(c + g) // LANES] = r.astype(BF16)
    ones_col = _ones_column((x_ref.shape[0], LANES))
    for c in range(0, d, nc):
        z = mm(2 * d + c, nc)
        for g in range(0, nc, LANES):
            v_ref[(c + g) // LANES, :, 0:LANES] = z[:, g:g + LANES].astype(BF16)
            v_ref[(c + g) // LANES, :, LANES:2 * LANES] = ones_col
    zm_ref[...] = mm(3 * d, zm_w)
    for c in range(0, g_w, nc):
        g_ref[:, c:c + nc] = mm(3 * d + zm_w + c, nc)


def _head_major_spec(heads, tm, w, seq):
    per_seq = seq // tm
    return pl.BlockSpec((None, heads, tm, w), lambda i: (i // per_seq, 0, i % per_seq, 0))


def _in_proj(x2, ln, w_p, cos_t, sin_t, *, bsz, seq, d, zm_w, g_w, tm, q_scale):
    t = x2.shape[0]
    row = lambda w: pl.BlockSpec((tm, w), lambda i: (i, 0))
    heads = d // LANES
    hm = _head_major_spec(heads, tm, LANES, seq)
    hm_v = _head_major_spec(heads, tm, 2 * LANES, seq)
    kern = functools.partial(_in_proj_kernel, d=d, zm_w=zm_w, g_w=g_w, nc=512, q_scale=q_scale)
    return pl.pallas_call(
        kern,
        grid=(t // tm,),
        in_specs=[row(d), _resident((1, d)), _resident(w_p.shape), row(LANES), row(LANES)],
        out_specs=[hm, hm, hm_v, row(zm_w), row(g_w)],
        out_shape=[jax.ShapeDtypeStruct((bsz, heads, seq, LANES), BF16)] * 2
        + [jax.ShapeDtypeStruct((bsz, heads, seq, 2 * LANES), BF16),
           jax.ShapeDtypeStruct((t, zm_w), F32), jax.ShapeDtypeStruct((t, g_w), F32)],
        compiler_params=pltpu.CompilerParams(
            dimension_semantics=("parallel",), vmem_limit_bytes=VMEM_LIMIT),
        name="in_proj",
    )(x2, ln, w_p, cos_t, sin_t)


def _mla_up_kernel(zm_ref, qn_ref, kvn_ref, wq_ref, wkv_ref, cos_ref, sin_ref,
                   q_ref, k_ref, v_ref, *, q_scale):
    cos = cos_ref[...]
    sin = sin_ref[...]
    first = _first_half_mask(cos.shape)
    cq = _rms(zm_ref[:, 0:MLA_Q_RANK], qn_ref[...]).astype(BF16)
    ckv = _rms(zm_ref[:, MLA_Q_RANK:MLA_Q_RANK + MLA_KV_RANK], kvn_ref[...]).astype(BF16)
    kr0 = MLA_Q_RANK + MLA_KV_RANK
    k_rope = _rope_tile(zm_ref[:, kr0:kr0 + LANES], cos, sin, first).astype(BF16)
    ones_col = _ones_column(cos.shape)
    for hd in range(MLA_HEADS):
        c = hd * MLA_HEAD_PAD
        qh = jnp.dot(cq, wq_ref[:, c:c + MLA_HEAD_PAD], preferred_element_type=F32)
        q_ref[hd, :, 0:LANES] = (qh[:, :LANES] * q_scale).astype(BF16)
        q_ref[hd, :, LANES:2 * LANES] = (
            _rope_tile(qh[:, LANES:], cos, sin, first) * q_scale).astype(BF16)
        kvh = jnp.dot(ckv, wkv_ref[:, c:c + MLA_HEAD_PAD], preferred_element_type=F32)
        k_ref[hd, :, 0:LANES] = kvh[:, :LANES].astype(BF16)
        k_ref[hd, :, LANES:2 * LANES] = k_rope
        v_ref[hd, :, 0:LANES] = kvh[:, LANES:].astype(BF16)
        v_ref[hd, :, LANES:2 * LANES] = ones_col


def _mla_up(zm, qn, kvn, wq_p, wkv, cos_t, sin_t, *, bsz, seq, tm, q_scale):
    t, zm_w = zm.shape
    row = lambda w: pl.BlockSpec((tm, w), lambda i: (i, 0))
    qk = _head_major_spec(MLA_HEADS, tm, MLA_HEAD_PAD, seq)
    vv = _head_major_spec(MLA_HEADS, tm, MLA_V_DIM + LANES, seq)
    return pl.pallas_call(
        functools.partial(_mla_up_kernel, q_scale=q_scale),
        grid=(t // tm,),
        in_specs=[row(zm_w), _resident(qn.shape), _resident(kvn.shape),
                  _resident(wq_p.shape), _resident(wkv.shape), row(LANES), row(LANES)],
        out_specs=[qk, qk, vv],
        out_shape=[jax.ShapeDtypeStruct((bsz, MLA_HEADS, seq, MLA_HEAD_PAD), BF16)] * 2
        + [jax.ShapeDtypeStruct((bsz, MLA_HEADS, seq, MLA_V_DIM + LANES), BF16)],
        compiler_params=pltpu.CompilerParams(
            dimension_semantics=("parallel",), vmem_limit_bytes=VMEM_LIMIT),
        name="mla_up",
    )(zm, qn, kvn, wq_p, wkv, cos_t, sin_t)


def _flash_kernel(q_ref, qn_ref, k_ref, v_ref, *rest, tq, dual, lambda_init):
    if dual:
        lam_ref, subln_ref, o_ref, qz_sc, sa_sc, sb_sc, m_sc, acc_sc = rest
    else:
        o_ref, sa_sc, sb_sc, m_sc, acc_sc = rest
    i = pl.program_id(2)
    dv = o_ref.shape[-1]
    rows = m_sc.shape[0]
    half = rows // 2
    wp = tq // 2
    nt = (((1,), (1,)), ((), ()))

    def stack(q_in):
        lo = lax.broadcasted_iota(jnp.int32, q_in.shape, 1) < DA_HEAD_DIM
        zero = jnp.zeros_like(q_in)
        q1 = jnp.where(lo, q_in, zero)
        q2 = jnp.where(lo, zero, q_in)
        return jnp.concatenate([q1[0:wp], q2[0:wp], q1[wp:], q2[wp:]], axis=0)

    if dual:
        qz_sc[...] = stack(q_ref[...])
    m_sc[...] = jnp.full(m_sc.shape, -jnp.inf, F32)
    acc_sc[...] = jnp.zeros(acc_sc.shape, F32)
    q_src = qz_sc if dual else q_ref

    def qk(s_ref, key0, width=tq, r0=0, c0=0):
        kk = k_ref[pl.ds(pl.multiple_of(key0, wp), width), :]
        s_ref[r0:rows, c0:c0 + width] = lax.dot_general(
            q_src[r0:rows, :], kk, nt, preferred_element_type=F32)

    def qk_next():
        qn = stack(qn_ref[...]) if dual else qn_ref[...]
        sa_sc[...] = lax.dot_general(qn, k_ref[0:tq, :], nt, preferred_element_type=F32)

    def softmax_pv(s_ref, key0, width=tq, r0=0, c0=0, piece=0):
        s = s_ref[r0:rows, c0:c0 + width]
        if piece:
            r = lax.broadcasted_iota(jnp.int32, s.shape, 0) + r0
            c = lax.broadcasted_iota(jnp.int32, s.shape, 1)
            q_chunk = (r & (wp - 1)) >> CHUNK_SHIFT
            if piece == 1:
                q_chunk = q_chunk + (r >> (half.bit_length() - 1)) * (wp // CHUNK)
            s = jnp.where((c >> CHUNK_SHIFT) <= q_chunk, s, MASK_VALUE)
        m_prev = m_sc[r0:rows, :]
        m_new = jnp.maximum(m_prev, jnp.max(s, axis=1, keepdims=True))
        alpha = jnp.exp2(m_prev - m_new)
        p = jnp.exp2((s - jnp.concatenate([m_new] * (width // LANES), axis=1)).astype(BF16))
        vv = v_ref[pl.ds(pl.multiple_of(key0, wp), width), :]
        pv = jnp.dot(p, vv, preferred_element_type=F32)
        alpha_w = jnp.concatenate([alpha] * (acc_sc.shape[1] // LANES), axis=1)
        acc_sc[r0:rows, :] = alpha_w * acc_sc[r0:rows, :] + pv
        m_sc[r0:rows, :] = m_new

    @pl.when(i == 0)
    def _first_block_of_head():
        qk(sa_sc, 0)

    def pair(t, carry):
        qk(sb_sc, (2 * t + 1) * tq)
        softmax_pv(sa_sc, 2 * t * tq)
        qk(sa_sc, (2 * t + 2) * tq)
        softmax_pv(sb_sc, (2 * t + 1) * tq)
        return carry

    lax.fori_loop(0, i // 2, pair, 0)

    d0 = i * tq

    def finalize():
        o = acc_sc[:, 0:dv] / acc_sc[:, dv:dv + 1]
        if dual:
            lp = lam_ref[...]
            lam = (jnp.exp(jnp.sum(lp[0:1] * lp[1:2], axis=1, keepdims=True))
                   - jnp.exp(jnp.sum(lp[2:3] * lp[3:4], axis=1, keepdims=True))
                   + lambda_init)
            o1 = jnp.concatenate([o[0:wp], o[2 * wp:3 * wp]], axis=0)
            o2 = jnp.concatenate([o[wp:2 * wp], o[3 * wp:]], axis=0)
            oa = o1 - lam * o2
            o_ref[...] = (_rms(oa, subln_ref[...]) * (1.0 - lambda_init)).astype(BF16)
        else:
            o_ref[...] = o.astype(BF16)

    @pl.when(i % 2 == 0)
    def _even_tail():
        softmax_pv(sa_sc, d0, wp, piece=1)
        softmax_pv(sa_sc, d0 + wp, wp, r0=half, c0=wp, piece=2)
        qk_next()
        finalize()

    @pl.when(i % 2 == 1)
    def _odd_tail():
        qk(sb_sc, d0, wp)
        qk(sb_sc, d0 + wp, wp, r0=half, c0=wp)
        softmax_pv(sa_sc, d0 - tq)
        qk_next()
        softmax_pv(sb_sc, d0, wp, piece=1)
        softmax_pv(sb_sc, d0 + wp, wp, r0=half, c0=wp, piece=2)
        finalize()


def _flash(q, k, v, extras, *, blk, dual, lambda_init=0.0):
    bsz, heads, seq, dk = q.shape
    dvp = v.shape[-1]
    dv = dvp - LANES
    assert blk % CHUNK == 0 and blk & (blk - 1) == 0 and seq % blk == 0
    rows = 2 * blk if dual else blk
    nq = seq // blk
    qspec = pl.BlockSpec((None, None, blk, dk), lambda b, h, i: (b, h, i, 0))
    qnspec = pl.BlockSpec((None, None, blk, dk),
                          lambda b, h, i: (b, h, jnp.minimum(i + 1, nq - 1), 0))
    kspec = pl.BlockSpec((None, None, seq, dk), lambda b, h, i: (b, h, 0, 0))
    vspec = pl.BlockSpec((None, None, seq, dvp), lambda b, h, i: (b, h, 0, 0))
    ospec = pl.BlockSpec((None, blk, dv), lambda b, h, i: (b, i, h))
    espec = [pl.BlockSpec(e.shape, lambda b, h, i: (0, 0)) for e in extras]
    scratch = [pltpu.VMEM((rows, blk), F32), pltpu.VMEM((rows, blk), F32),
               pltpu.VMEM((rows, LANES), F32), pltpu.VMEM((rows, dvp), F32)]
    if dual:
        scratch = [pltpu.VMEM((rows, dk), BF16)] + scratch
    kern = functools.partial(_flash_kernel, tq=blk, dual=dual, lambda_init=lambda_init)
    return pl.pallas_call(
        kern,
        grid=(bsz, heads, nq),
        in_specs=[qspec, qnspec, kspec, vspec] + espec,
        out_specs=ospec,
        scratch_shapes=scratch,
        out_shape=jax.ShapeDtypeStruct((bsz, seq, heads * dv), BF16),
        compiler_params=pltpu.CompilerParams(
            dimension_semantics=("parallel", "parallel", "arbitrary"),
            vmem_limit_bytes=VMEM_LIMIT),
        name="flash_da" if dual else "flash_mla",
    )(q, q, k, v, *extras)


def _post_mix_kernel(oa_ref, ob_ref, g_ref, x_ref, wa_ref, wb_ref, wo_ref, ln_ref, o_ref, *, d):
    y_a = jnp.dot(oa_ref[...], wa_ref[...], preferred_element_type=F32)
    y_b = jnp.dot(ob_ref[...], wb_ref[...], preferred_element_type=F32)
    merged = jax.nn.sigmoid(g_ref[:, 0:d]) * y_a + jax.nn.sigmoid(g_ref[:, d:2 * d]) * y_b
    m = jnp.dot(merged.astype(BF16), wo_ref[...], preferred_element_type=F32)
    o_ref[...] = x_ref[...] + _rms(m, ln_ref[...])


def _post_mix(oa, ob, g, x2, wa, wb, wo, ln, *, tm):
    t, d = x2.shape
    row = lambda w: pl.BlockSpec((tm, w), lambda i: (i, 0))
    return pl.pallas_call(
        functools.partial(_post_mix_kernel, d=d),
        grid=(t // tm,),
        in_specs=[row(d), row(d), row(2 * d), row(d), _resident(wa.shape), _resident(wb.shape),
                  _resident(wo.shape), _resident(ln.shape)],
        out_specs=row(d),
        out_shape=jax.ShapeDtypeStruct((t, d), F32),
        compiler_params=pltpu.CompilerParams(
            dimension_semantics=("parallel",), vmem_limit_bytes=VMEM_LIMIT),
        name="post_mix",
    )(oa, ob, g, x2, wa, wb, wo, ln)


def _ffn_kernel(x_ref, lnp_ref, wg_ref, wu_ref, wd_ref, lno_ref, o_ref):
    x = x_ref[...]
    h = _rms(x, lnp_ref[...]).astype(BF16)
    gate = jnp.dot(h, wg_ref[...], preferred_element_type=F32)
    up = jnp.dot(h, wu_ref[...], preferred_element_type=F32)
    a = (jax.nn.silu(gate) * up).astype(BF16)
    f = jnp.dot(a, wd_ref[...], preferred_element_type=F32)
    o_ref[...] = x + _rms(f, lno_ref[...])


def _ffn(x2, lnp, wg, wu, wd, lno, *, tm):
    t, d = x2.shape
    row = pl.BlockSpec((tm, d), lambda i: (i, 0))
    return pl.pallas_call(
        _ffn_kernel,
        grid=(t // tm,),
        in_specs=[row, _resident(lnp.shape), _resident(wg.shape), _resident(wu.shape),
                  _resident(wd.shape), _resident(lno.shape)],
        out_specs=row,
        out_shape=jax.ShapeDtypeStruct((t, d), F32),
        compiler_params=pltpu.CompilerParams(
            dimension_semantics=("parallel",), vmem_limit_bytes=VMEM_LIMIT),
        name="ffn",
    )(x2, lnp, wg, wu, wd, lno)


def _pack_w_in(w, d):
    lat = MLA_Q_RANK + MLA_KV_RANK + ROPE_DIM
    zm_w = -(-lat // LANES) * LANES
    head, latent, gates = w[:, :3 * d], w[:, 3 * d:3 * d + lat], w[:, 3 * d + lat:]
    latent = jnp.pad(latent, ((0, 0), (0, zm_w - lat)))
    return jnp.concatenate([head, latent, gates], axis=1).astype(BF16), zm_w


def _pack_w_uq(w):
    r = w.shape[0]
    w = w.reshape(r, MLA_HEADS, MLA_QK_DIM)
    w = jnp.pad(w, ((0, 0), (0, 0), (0, MLA_HEAD_PAD - MLA_QK_DIM)))
    return w.reshape(r, MLA_HEADS * MLA_HEAD_PAD).astype(BF16)


def kernel(x, positions, ln_mix_pre, w_in, lambda_q1, lambda_k1, lambda_q2, lambda_k2, da_subln,
           q_a_norm, w_uq, kv_a_norm, w_ukv, w_proj_a, w_proj_b, w_o, ln_mix_post, ln_ffn_pre,
           w_ffn_gate, w_ffn_up, w_ffn_down, ln_ffn_post):
    bsz, seq, d = x.shape
    t = bsz * seq
    depth = w_in.shape[0]
    tm = 512

    half = ROPE_DIM // 2
    inv_freq = 1.0 / (ROPE_THETA ** (jnp.arange(half, dtype=F32) * (2.0 / ROPE_DIM)))
    inv_freq_lanes = jnp.tile(inv_freq, LANES // half).reshape(1, LANES)
    pos_lanes = jnp.broadcast_to(positions.reshape(t, 1), (t, LANES))
    cos_t, sin_t = _rope_tables(pos_lanes, inv_freq_lanes, 1024)

    x2 = x.reshape(t, d)
    for l in range(depth):
        lambda_init = 0.8 - 0.6 * math.exp(-0.3 * l)
        w_p, zm_w = _pack_w_in(w_in[l], d)
        q_da, k_da, v_da, zm, gates = _in_proj(
            x2, ln_mix_pre[l].reshape(1, d), w_p, cos_t, sin_t,
            bsz=bsz, seq=seq, d=d, zm_w=zm_w, g_w=2 * d, tm=tm,
            q_scale=DA_HEAD_DIM ** -0.5 * LOG2E)
        q_b, k_b, v_b = _mla_up(
            zm, q_a_norm[l].reshape(1, -1), kv_a_norm[l].reshape(1, -1),
            _pack_w_uq(w_uq[l]), w_ukv[l].astype(BF16), cos_t, sin_t,
            bsz=bsz, seq=seq, tm=tm, q_scale=MLA_QK_DIM ** -0.5 * LOG2E)

        lam_p = jnp.stack([lambda_q1[l], lambda_k1[l], lambda_q2[l], lambda_k2[l]]).astype(F32)
        lam_p = jnp.pad(lam_p, ((0, 0), (0, LANES - DA_HEAD_DIM)))
        oa = _flash(q_da, k_da, v_da, [lam_p, da_subln[l].reshape(1, -1)], blk=1024, dual=True,
                    lambda_init=lambda_init)
        ob = _flash(q_b, k_b, v_b, [], blk=1024, dual=False)

        x2 = _post_mix(oa.reshape(t, d), ob.reshape(t, d), gates, x2,
                       w_proj_a[l].astype(BF16), w_proj_b[l].astype(BF16), w_o[l].astype(BF16),
                       ln_mix_post[l].reshape(1, d), tm=tm)
        x2 = _ffn(x2, ln_ffn_pre[l].reshape(1, d), w_ffn_gate[l].astype(BF16),
                  w_ffn_up[l].astype(BF16), w_ffn_down[l].astype(BF16),
                  ln_ffn_post[l].reshape(1, d), tm=tm)
    return x2.reshape(bsz, seq, d)
```

```python
import functools
import math

import jax
import jax.numpy as jnp
from jax import lax
from jax.experimental import pallas as pl
from jax.experimental.pallas import tpu as pltpu

F32 = jnp.float32
BF16 = jnp.bfloat16

LANES = 128
CHUNK = 64
CHUNK_SHIFT = 6
RMS_EPS = 1e-6
ROPE_THETA = 10000.0
ROPE_DIM = 64
MASK_VALUE = -1e30
LOG2E = math.log2(math.e)

DA_HEADS = 8
DA_HEAD_DIM = 64
MLA_HEADS = 8
MLA_Q_RANK = 384
MLA_KV_RANK = 256
MLA_NOPE_DIM = 128
MLA_V_DIM = 128
MLA_QK_DIM = MLA_NOPE_DIM + ROPE_DIM
MLA_HEAD_PAD = 256

VMEM_LIMIT = 56 * 1024 * 1024


def _rms(x, w, eps=RMS_EPS):
    return x * lax.rsqrt(jnp.mean(x * x, axis=-1, keepdims=True) + eps) * w


def _rope_tile(x, cos, sin_signed, first_half):
    rot = jnp.where(first_half, pltpu.roll(x, 96, 1), pltpu.roll(x, 32, 1))
    return x * cos + rot * sin_signed


def _first_half_mask(shape):
    lane = lax.broadcasted_iota(jnp.int32, shape, 1)
    return (lane & (ROPE_DIM // 2)) == 0


def _ones_column(shape):
    lane = lax.broadcasted_iota(jnp.int32, shape, 1)
    return jnp.where(lane == 0, 1.0, 0.0).astype(BF16)


def _resident(shape):
    return pl.BlockSpec(shape, lambda *_: (0,) * len(shape), pipeline_mode=pl.Buffered(1))


def _rope_table_kernel(pos_ref, invf_ref, cos_ref, sin_ref):
    ang = pos_ref[...].astype(F32) * invf_ref[...]
    first = _first_half_mask(ang.shape)
    s = jnp.sin(ang)
    cos_ref[...] = jnp.cos(ang)
    sin_ref[...] = jnp.where(first, -s, s)


def _rope_tables(pos_lanes, inv_freq_lanes, tm):
    t = pos_lanes.shape[0]
    row = pl.BlockSpec((tm, LANES), lambda i: (i, 0))
    return pl.pallas_call(
        _rope_table_kernel,
        grid=(t // tm,),
        in_specs=[row, pl.BlockSpec((1, LANES), lambda i: (0, 0))],
        out_specs=[row, row],
        out_shape=[jax.ShapeDtypeStruct((t, LANES), F32)] * 2,
        compiler_params=pltpu.CompilerParams(dimension_semantics=("parallel",)),
        name="rope_tables",
    )(pos_lanes, inv_freq_lanes)


def _in_proj_kernel(x_ref, ln_ref, w_ref, cos_ref, sin_ref,
                    q_ref, k_ref, v_ref, zm_ref, g_ref, *, d, zm_w, g_w, nc, q_scale):
    h = _rms(x_ref[...], ln_ref[...]).astype(BF16)
    cos = cos_ref[...]
    sin = sin_ref[...]
    first = _first_half_mask(cos.shape)

    def mm(c0, width):
        return jnp.dot(h, w_ref[:, c0:c0 + width], preferred_element_type=F32)

    for c in range(0, d, nc):
        z = mm(c, nc)
        for g in range(0, nc, LANES):
            r = _rope_tile(z[:, g:g + LANES], cos, sin, first) * q_scale
            q_ref[(c + g) // LANES] = r.astype(BF16)
    for c in range(0, d, nc):
        z = mm(d + c, nc)
        for g in range(0, nc, LANES):
            r = _rope_tile(z[:, g:g + LANES], cos, sin, first)
            k_ref[(c + g) // LANES] = r.astype(BF16)
    ones_col = _ones_column((x_ref.shape[0], LANES))
    for c in range(0, d, nc):
        z = mm(2 * d + c, nc)
        for g in range(0, nc, LANES):
            v_ref[(c + g) // LANES, :, 0:LANES] = z[:, g:g + LANES].astype(BF16)
            v_ref[(c + g) // LANES, :, LANES:2 * LANES] = ones_col
    zm_ref[...] = mm(3 * d, zm_w)
    for c in range(0, g_w, nc):
        g_ref[:, c:c + nc] = mm(3 * d + zm_w + c, nc)


def _head_major_spec(heads, tm, w, seq):
    per_seq = seq // tm
    return pl.BlockSpec((None, heads, tm, w), lambda i: (i // per_seq, 0, i % per_seq, 0))


def _in_proj(x2, ln, w_p, cos_t, sin_t, *, bsz, seq, d, zm_w, g_w, tm, q_scale):
    t = x2.shape[0]
    row = lambda w: pl.BlockSpec((tm, w), lambda i: (i, 0))
    heads = d // LANES
    hm = _head_major_spec(heads, tm, LANES, seq)
    hm_v = _head_major_spec(heads, tm, 2 * LANES, seq)
    kern = functools.partial(_in_proj_kernel, d=d, zm_w=zm_w, g_w=g_w, nc=512, q_scale=q_scale)
    return pl.pallas_call(
        kern,
        grid=(t // tm,),
        in_specs=[row(d), _resident((1, d)), _resident(w_p.shape), row(LANES), row(LANES)],
        out_specs=[hm, hm, hm_v, row(zm_w), row(g_w)],
        out_shape=[jax.ShapeDtypeStruct((bsz, heads, seq, LANES), BF16)] * 2
        + [jax.ShapeDtypeStruct((bsz, heads, seq, 2 * LANES), BF16),
           jax.ShapeDtypeStruct((t, zm_w), F32), jax.ShapeDtypeStruct((t, g_w), F32)],
        compiler_params=pltpu.CompilerParams(
            dimension_semantics=("parallel",), vmem_limit_bytes=VMEM_LIMIT),
        name="in_proj",
    )(x2, ln, w_p, cos_t, sin_t)


def _mla_up_kernel(zm_ref, qn_ref, kvn_ref, wq_ref, wkv_ref, cos_ref, sin_ref,
                   q_ref, k_ref, v_ref, *, q_scale):
    cos = cos_ref[...]
    sin = sin_ref[...]
    first = _first_half_mask(cos.shape)
    cq = _rms(zm_ref[:, 0:MLA_Q_RANK], qn_ref[...]).astype(BF16)
    ckv = _rms(zm_ref[:, MLA_Q_RANK:MLA_Q_RANK + MLA_KV_RANK], kvn_ref[...]).astype(BF16)
    kr0 = MLA_Q_RANK + MLA_KV_RANK
    k_rope = _rope_tile(zm_ref[:, kr0:kr0 + LANES], cos, sin, first).astype(BF16)
    ones_col = _ones_column(cos.shape)
    for hd in range(MLA_HEADS):
        c = hd * MLA_HEAD_PAD
        qh = jnp.dot(cq, wq_ref[:, c:c + MLA_HEAD_PAD], preferred_element_type=F32)
        q_ref[hd, :, 0:LANES] = (qh[:, :LANES] * q_scale).astype(BF16)
        q_ref[hd, :, LANES:2 * LANES] = (
            _rope_tile(qh[:, LANES:], cos, sin, first) * q_scale).astype(BF16)
        kvh = jnp.dot(ckv, wkv_ref[:, c:c + MLA_HEAD_PAD], preferred_element_type=F32)
        k_ref[hd, :, 0:LANES] = kvh[:, :LANES].astype(BF16)
        k_ref[hd, :, LANES:2 * LANES] = k_rope
        v_ref[hd, :, 0:LANES] = kvh[:, LANES:].astype(BF16)
        v_ref[hd, :, LANES:2 * LANES] = ones_col


def _mla_up(zm, qn, kvn, wq_p, wkv, cos_t, sin_t, *, bsz, seq, tm, q_scale):
    t, zm_w = zm.shape
    row = lambda w: pl.BlockSpec((tm, w), lambda i: (i, 0))
    qk = _head_major_spec(MLA_HEADS, tm, MLA_HEAD_PAD, seq)
    vv = _head_major_spec(MLA_HEADS, tm, MLA_V_DIM + LANES, seq)
    return pl.pallas_call(
        functools.partial(_mla_up_kernel, q_scale=q_scale),
        grid=(t // tm,),
        in_specs=[row(zm_w), _resident(qn.shape), _resident(kvn.shape),
                  _resident(wq_p.shape), _resident(wkv.shape), row(LANES), row(LANES)],
        out_specs=[qk, qk, vv],
        out_shape=[jax.ShapeDtypeStruct((bsz, MLA_HEADS, seq, MLA_HEAD_PAD), BF16)] * 2
        + [jax.ShapeDtypeStruct((bsz, MLA_HEADS, seq, MLA_V_DIM + LANES), BF16)],
        compiler_params=pltpu.CompilerParams(
            dimension_semantics=("parallel",), vmem_limit_bytes=VMEM_LIMIT),
        name="mla_up",
    )(zm, qn, kvn, wq_p, wkv, cos_t, sin_t)


def _flash_kernel(q_ref, qn_ref, k_ref, v_ref, tri_ref, *rest, tq, dual, lambda_init):
    if dual:
        lam_ref, subln_ref, o_ref, qz_sc, sa_sc, sb_sc, m_sc, acc_sc = rest
    else:
        o_ref, sa_sc, sb_sc, m_sc, acc_sc = rest
    i = pl.program_id(2)
    dv = o_ref.shape[-1]
    rows = m_sc.shape[0]
    half = rows // 2
    wp = tq // 2
    nt = (((1,), (1,)), ((), ()))

    def stack(q_in):
        lo = lax.broadcasted_iota(jnp.int32, q_in.shape, 1) < DA_HEAD_DIM
        zero = jnp.zeros_like(q_in)
        q1 = jnp.where(lo, q_in, zero)
        q2 = jnp.where(lo, zero, q_in)
        return jnp.concatenate([q1[0:wp], q2[0:wp], q1[wp:], q2[wp:]], axis=0)

    if dual:
        qz_sc[...] = stack(q_ref[...])
    m_sc[...] = jnp.full(m_sc.shape, -jnp.inf, F32)
    acc_sc[...] = jnp.zeros(acc_sc.shape, F32)
    q_src = qz_sc if dual else q_ref

    def qk(s_ref, key0, width=tq, r0=0, r1=rows):
        kk = k_ref[pl.ds(pl.multiple_of(key0, wp), width), :]
        s_ref[r0:r1, 0:width] = lax.dot_general(
            q_src[r0:r1, :], kk, nt, preferred_element_type=F32)

    def qk_next():
        qn = stack(qn_ref[...]) if dual else qn_ref[...]
        sa_sc[...] = lax.dot_general(qn, k_ref[0:tq, :], nt, preferred_element_type=F32)

    def chunk_causal(s):
        allowed = tri_ref[...] != 0.0
        groups = [jnp.where(allowed, s[g:g + wp], MASK_VALUE) for g in range(0, s.shape[0], wp)]
        return jnp.concatenate(groups, axis=0)

    def softmax_pv(s_ref, key0, width=tq, r0=0, r1=rows, diag=False):
        s = s_ref[r0:r1, 0:width]
        if diag:
            tail = chunk_causal(s[:, width - wp:])
            s = tail if width == wp else jnp.concatenate([s[:, 0:width - wp], tail], axis=1)
        m_prev = m_sc[r0:r1, :]
        m_new = jnp.maximum(m_prev, jnp.max(s, axis=1, keepdims=True))
        alpha = jnp.exp2(m_prev - m_new)
        p = jnp.exp2((s - jnp.concatenate([m_new] * (width // LANES), axis=1)).astype(BF16))
        vv = v_ref[pl.ds(pl.multiple_of(key0, wp), width), :]
        pv = jnp.dot(p, vv, preferred_element_type=F32)
        alpha_w = jnp.concatenate([alpha] * (acc_sc.shape[1] // LANES), axis=1)
        acc_sc[r0:r1, :] = alpha_w * acc_sc[r0:r1, :] + pv
        m_sc[r0:r1, :] = m_new

    def diag_block(s_ref):
        softmax_pv(s_ref, d0, wp, 0, half, diag=True)
        softmax_pv(s_ref, d0, tq, half, rows, diag=True)

    @pl.when(i == 0)
    def _first_block_of_head():
        qk(sa_sc, 0)

    def pair(t, carry):
        qk(sb_sc, (2 * t + 1) * tq)
        softmax_pv(sa_sc, 2 * t * tq)
        qk(sa_sc, (2 * t + 2) * tq)
        softmax_pv(sb_sc, (2 * t + 1) * tq)
        return carry

    lax.fori_loop(0, i // 2, pair, 0)

    d0 = i * tq

    def finalize():
        o = acc_sc[:, 0:dv] / acc_sc[:, dv:dv + 1]
        if dual:
            lp = lam_ref[...]
            lam = (jnp.exp(jnp.sum(lp[0:1] * lp[1:2], axis=1, keepdims=True))
                   - jnp.exp(jnp.sum(lp[2:3] * lp[3:4], axis=1, keepdims=True))
                   + lambda_init)
            o1 = jnp.concatenate([o[0:wp], o[2 * wp:3 * wp]], axis=0)
            o2 = jnp.concatenate([o[wp:2 * wp], o[3 * wp:]], axis=0)
            oa = o1 - lam * o2
            o_ref[...] = (_rms(oa, subln_ref[...]) * (1.0 - lambda_init)).astype(BF16)
        else:
            o_ref[...] = o.astype(BF16)

    @pl.when(i % 2 == 0)
    def _even_tail():
        diag_block(sa_sc)
        qk_next()
        finalize()

    @pl.when(i % 2 == 1)
    def _odd_tail():
        qk(sb_sc, d0, wp, 0, half)
        qk(sb_sc, d0, tq, half, rows)
        softmax_pv(sa_sc, d0 - tq)
        qk_next()
        diag_block(sb_sc)
        finalize()


def _flash(q, k, v, extras, *, blk, dual, lambda_init=0.0):
    bsz, heads, seq, dk = q.shape
    dvp = v.shape[-1]
    dv = dvp - LANES
    assert blk % CHUNK == 0 and blk & (blk - 1) == 0 and seq % blk == 0
    rows = 2 * blk if dual else blk
    nq = seq // blk
    qspec = pl.BlockSpec((None, None, blk, dk), lambda b, h, i: (b, h, i, 0))
    qnspec = pl.BlockSpec((None, None, blk, dk),
                          lambda b, h, i: (b, h, jnp.minimum(i + 1, nq - 1), 0))
    kspec = pl.BlockSpec((None, None, seq, dk), lambda b, h, i: (b, h, 0, 0))
    vspec = pl.BlockSpec((None, None, seq, dvp), lambda b, h, i: (b, h, 0, 0))
    ospec = pl.BlockSpec((None, blk, dv), lambda b, h, i: (b, i, h))
    wp = blk // 2
    frame_chunk = jnp.arange(wp, dtype=jnp.int32) // CHUNK
    tri = (frame_chunk[None, :] <= frame_chunk[:, None]).astype(F32)
    extras = [tri] + list(extras)
    espec = [pl.BlockSpec(e.shape, lambda b, h, i: (0, 0)) for e in extras]
    scratch = [pltpu.VMEM((rows, blk), F32), pltpu.VMEM((rows, blk), F32),
               pltpu.VMEM((rows, LANES), F32), pltpu.VMEM((rows, dvp), F32)]
    if dual:
        scratch = [pltpu.VMEM((rows, dk), BF16)] + scratch
    kern = functools.partial(_flash_kernel, tq=blk, dual=dual, lambda_init=lambda_init)
    return pl.pallas_call(
        kern,
        grid=(bsz, heads, nq),
        in_specs=[qspec, qnspec, kspec, vspec] + espec,
        out_specs=ospec,
        scratch_shapes=scratch,
        out_shape=jax.ShapeDtypeStruct((bsz, seq, heads * dv), BF16),
        compiler_params=pltpu.CompilerParams(
            dimension_semantics=("parallel", "parallel", "arbitrary"),
            vmem_limit_bytes=VMEM_LIMIT),
        name="flash_da" if dual else "flash_mla",
    )(q, q, k, v, *extras)


def _post_mix_kernel(oa_ref, ob_ref, g_ref, x_ref, wa_ref, wb_ref, wo_ref, ln_ref, o_ref, *, d):
    y_a = jnp.dot(oa_ref[...], wa_ref[...], preferred_element_type=F32)
    y_b = jnp.dot(ob_ref[...], wb_ref[...], preferred_element_type=F32)
    merged = jax.nn.sigmoid(g_ref[:, 0:d]) * y_a + jax.nn.sigmoid(g_ref[:, d:2 * d]) * y_b
    m = jnp.dot(merged.astype(BF16), wo_ref[...], preferred_element_type=F32)
    o_ref[...] = x_ref[...] + _rms(m, ln_ref[...])


def _post_mix(oa, ob, g, x2, wa, wb, wo, ln, *, tm):
    t, d = x2.shape
    row = lambda w: pl.BlockSpec((tm, w), lambda i: (i, 0))
    return pl.pallas_call(
        functools.partial(_post_mix_kernel, d=d),
        grid=(t // tm,),
        in_specs=[row(d), row(d), row(2 * d), row(d), _resident(wa.shape), _resident(wb.shape),
                  _resident(wo.shape), _resident(ln.shape)],
        out_specs=row(d),
        out_shape=jax.ShapeDtypeStruct((t, d), F32),
        compiler_params=pltpu.CompilerParams(
            dimension_semantics=("parallel",), vmem_limit_bytes=VMEM_LIMIT),
        name="post_mix",
    )(oa, ob, g, x2, wa, wb, wo, ln)


def _ffn_kernel(x_ref, lnp_ref, wg_ref, wu_ref, wd_ref, lno_ref, o_ref):
    x = x_ref[...]
    h = _rms(x, lnp_ref[...]).astype(BF16)
    gate = jnp.dot(h, wg_ref[...], preferred_element_type=F32)
    up = jnp.dot(h, wu_ref[...], preferred_element_type=F32)
    a = (jax.nn.silu(gate) * up).astype(BF16)
    f = jnp.dot(a, wd_ref[...], preferred_element_type=F32)
    o_ref[...] = x + _rms(f, lno_ref[...])


def _ffn(x2, lnp, wg, wu, wd, lno, *, tm):
    t, d = x2.shape
    row = pl.BlockSpec((tm, d), lambda i: (i, 0))
    return pl.pallas_call(
        _ffn_kernel,
        grid=(t // tm,),
        in_specs=[row, _resident(lnp.shape), _resident(wg.shape), _resident(wu.shape),
                  _resident(wd.shape), _resident(lno.shape)],
        out_specs=row,
        out_shape=jax.ShapeDtypeStruct((t, d), F32),
        compiler_params=pltpu.CompilerParams(
            dimension_semantics=("parallel",), vmem_limit_bytes=VMEM_LIMIT),
        name="ffn",
    )(x2, lnp, wg, wu, wd, lno)


def _pack_w_in(w, d):
    lat = MLA_Q_RANK + MLA_KV_RANK + ROPE_DIM
    zm_w = -(-lat // LANES) * LANES
    head, latent, gates = w[:, :3 * d], w[:, 3 * d:3 * d + lat], w[:, 3 * d + lat:]
    latent = jnp.pad(latent, ((0, 0), (0, zm_w - lat)))
    return jnp.concatenate([head, latent, gates], axis=1).astype(BF16), zm_w


def _pack_w_uq(w):
    r = w.shape[0]
    w = w.reshape(r, MLA_HEADS, MLA_QK_DIM)
    w = jnp.pad(w, ((0, 0), (0, 0), (0, MLA_HEAD_PAD - MLA_QK_DIM)))
    return w.reshape(r, MLA_HEADS * MLA_HEAD_PAD).astype(BF16)


def kernel(x, positions, ln_mix_pre, w_in, lambda_q1, lambda_k1, lambda_q2, lambda_k2, da_subln,
           q_a_norm, w_uq, kv_a_norm, w_ukv, w_proj_a, w_proj_b, w_o, ln_mix_post, ln_ffn_pre,
           w_ffn_gate, w_ffn_up, w_ffn_down, ln_ffn_post):
    bsz, seq, d = x.shape
    t = bsz * seq
    depth = w_in.shape[0]
    tm = 512

    half = ROPE_DIM // 2
    inv_freq = 1.0 / (ROPE_THETA ** (jnp.arange(half, dtype=F32) * (2.0 / ROPE_DIM)))
    inv_freq_lanes = jnp.tile(inv_freq, LANES // half).reshape(1, LANES)
    pos_lanes = jnp.broadcast_to(positions.reshape(t, 1), (t, LANES))
    cos_t, sin_t = _rope_tables(pos_lanes, inv_freq_lanes, 1024)

    x2 = x.reshape(t, d)
    for l in range(depth):
        lambda_init = 0.8 - 0.6 * math.exp(-0.3 * l)
        w_p, zm_w = _pack_w_in(w_in[l], d)
        q_da, k_da, v_da, zm, gates = _in_proj(
            x2, ln_mix_pre[l].reshape(1, d), w_p, cos_t, sin_t,
            bsz=bsz, seq=seq, d=d, zm_w=zm_w, g_w=2 * d, tm=tm,
            q_scale=DA_HEAD_DIM ** -0.5 * LOG2E)
        q_b, k_b, v_b = _mla_up(
            zm, q_a_norm[l].reshape(1, -1), kv_a_norm[l].reshape(1, -1),
            _pack_w_uq(w_uq[l]), w_ukv[l].astype(BF16), cos_t, sin_t,
            bsz=bsz, seq=seq, tm=tm, q_scale=MLA_QK_DIM ** -0.5 * LOG2E)

        lam_p = jnp.stack([lambda_q1[l], lambda_k1[l], lambda_q2[l], lambda_k2[l]]).astype(F32)
        lam_p = jnp.pad(lam_p, ((0, 0), (0, LANES - DA_HEAD_DIM)))
        oa = _flash(q_da, k_da, v_da, [lam_p, da_subln[l].reshape(1, -1)], blk=1024, dual=True,
                    lambda_init=lambda_init)
        ob = _flash(q_b, k_b, v_b, [], blk=1024, dual=False)

        x2 = _post_mix(oa.reshape(t, d), ob.reshape(t, d), gates, x2,
                       w_proj_a[l].astype(BF16), w_proj_b[l].astype(BF16), w_o[l].astype(BF16),
                       ln_mix_post[l].reshape(1, d), tm=tm)
        x2 = _ffn(x2, ln_ffn_pre[l].reshape(1, d), w_ffn_gate[l].astype(BF16),
                  w_ffn_up[l].astype(BF16), w_ffn_down[l].astype(BF16),
                  ln_ffn_post[l].reshape(1, d), tm=tm)
    return x2.reshape(bsz, seq, d)
```

```python
import functools
import math

import jax
import jax.numpy as jnp
from jax import lax
from jax.experimental import pallas as pl
from jax.experimental.pallas import tpu as pltpu

F32 = jnp.float32
BF16 = jnp.bfloat16

LANES = 128
CHUNK = 64
RMS_EPS = 1e-6
ROPE_THETA = 10000.0
ROPE_DIM = 64
MASK_VALUE = -1e30
LOG2E = math.log2(math.e)

DA_HEAD_DIM = 64
MLA_HEADS = 8
MLA_Q_RANK = 384
MLA_KV_RANK = 256
MLA_NOPE_DIM = 128
MLA_V_DIM = 128
MLA_QK_DIM = MLA_NOPE_DIM + ROPE_DIM
MLA_HEAD_PAD = 256

VMEM_LIMIT = 56 * 1024 * 1024


def _rms(x, w, eps=RMS_EPS):
    return x * lax.rsqrt(jnp.mean(x * x, axis=-1, keepdims=True) + eps) * w


def _rope_tile(x, cos, sin_signed, first_half):
    rot = jnp.where(first_half, pltpu.roll(x, 96, 1), pltpu.roll(x, 32, 1))
    return x * cos + rot * sin_signed


def _first_half_mask(shape):
    lane = lax.broadcasted_iota(jnp.int32, shape, 1)
    return (lane & (ROPE_DIM // 2)) == 0


def _ones_column(shape):
    lane = lax.broadcasted_iota(jnp.int32, shape, 1)
    return jnp.where(lane == 0, 1.0, 0.0).astype(BF16)


def _resident(shape):
    return pl.BlockSpec(shape, lambda *_: (0,) * len(shape), pipeline_mode=pl.Buffered(1))


def _front_kernel(x_ref, pos_ref, invf_ref, ln_ref, w_ref, qn_ref, kvn_ref, wq_ref, wkv_ref,
                  qa_ref, ka_ref, va_ref, g_ref, qb_ref, kb_ref, vb_ref,
                  *, d, zm_w, nc, qa_scale, qb_scale):
    h = _rms(x_ref[...], ln_ref[...]).astype(BF16)
    ang = pos_ref[...].astype(F32) * invf_ref[...]
    first = _first_half_mask(ang.shape)
    cos = jnp.cos(ang)
    sin_plain = jnp.sin(ang)
    sin = jnp.where(first, -sin_plain, sin_plain)

    def mm(c0, width):
        return jnp.dot(h, w_ref[:, c0:c0 + width], preferred_element_type=F32)

    for c in range(0, d, nc):
        z = mm(c, nc)
        for g in range(0, nc, LANES):
            r = _rope_tile(z[:, g:g + LANES], cos, sin, first) * qa_scale
            qa_ref[(c + g) // LANES] = r.astype(BF16)
    for c in range(0, d, nc):
        z = mm(d + c, nc)
        for g in range(0, nc, LANES):
            r = _rope_tile(z[:, g:g + LANES], cos, sin, first)
            ka_ref[(c + g) // LANES] = r.astype(BF16)
    for c in range(0, d, nc):
        z = mm(2 * d + c, nc)
        for g in range(0, nc, LANES):
            va_ref[(c + g) // LANES] = z[:, g:g + LANES].astype(BF16)
    for c in range(0, 2 * d, nc):
        g_ref[:, c:c + nc] = jax.nn.sigmoid(mm(3 * d + zm_w + c, nc)).astype(BF16)

    zm = mm(3 * d, zm_w)
    cq = _rms(zm[:, 0:MLA_Q_RANK], qn_ref[...]).astype(BF16)
    ckv = _rms(zm[:, MLA_Q_RANK:MLA_Q_RANK + MLA_KV_RANK], kvn_ref[...]).astype(BF16)
    kr0 = MLA_Q_RANK + MLA_KV_RANK
    k_rope = _rope_tile(zm[:, kr0:kr0 + LANES], cos, sin, first).astype(BF16)
    for hd in range(MLA_HEADS):
        c = hd * MLA_HEAD_PAD
        qh = jnp.dot(cq, wq_ref[:, c:c + MLA_HEAD_PAD], preferred_element_type=F32)
        qb_ref[hd, :, 0:LANES] = (qh[:, :LANES] * qb_scale).astype(BF16)
        qb_ref[hd, :, LANES:2 * LANES] = (
            _rope_tile(qh[:, LANES:], cos, sin, first) * qb_scale).astype(BF16)
        kvh = jnp.dot(ckv, wkv_ref[:, c:c + MLA_HEAD_PAD], preferred_element_type=F32)
        kb_ref[hd, :, 0:LANES] = kvh[:, :LANES].astype(BF16)
        kb_ref[hd, :, LANES:2 * LANES] = k_rope
        vb_ref[hd] = kvh[:, LANES:].astype(BF16)


def _head_major_spec(heads, tm, w, seq):
    per_seq = seq // tm
    return pl.BlockSpec((None, heads, tm, w), lambda i: (i // per_seq, 0, i % per_seq, 0))


def _front(x2, pos_lanes, inv_freq_lanes, ln, w_p, qn, kvn, wq_p, wkv,
           *, bsz, seq, d, zm_w, tm, qa_scale, qb_scale):
    t = x2.shape[0]
    row = lambda w: pl.BlockSpec((tm, w), lambda i: (i, 0))
    da_heads = d // LANES
    hm = lambda heads, w: _head_major_spec(heads, tm, w, seq)
    hm_shape = lambda heads, w: jax.ShapeDtypeStruct((bsz, heads, seq, w), BF16)
    kern = functools.partial(_front_kernel, d=d, zm_w=zm_w, nc=512,
                             qa_scale=qa_scale, qb_scale=qb_scale)
    return pl.pallas_call(
        kern,
        grid=(t // tm,),
        in_specs=[row(d), row(LANES), _resident((1, LANES)), _resident((1, d)),
                  _resident(w_p.shape), _resident(qn.shape), _resident(kvn.shape),
                  _resident(wq_p.shape), _resident(wkv.shape)],
        out_specs=[hm(da_heads, LANES), hm(da_heads, LANES), hm(da_heads, LANES), row(2 * d),
                   hm(MLA_HEADS, MLA_HEAD_PAD), hm(MLA_HEADS, MLA_HEAD_PAD),
                   hm(MLA_HEADS, MLA_V_DIM)],
        out_shape=[hm_shape(da_heads, LANES)] * 3 + [jax.ShapeDtypeStruct((t, 2 * d), BF16)]
        + [hm_shape(MLA_HEADS, MLA_HEAD_PAD)] * 2 + [hm_shape(MLA_HEADS, MLA_V_DIM)],
        compiler_params=pltpu.CompilerParams(
            dimension_semantics=("parallel",), vmem_limit_bytes=VMEM_LIMIT),
        name="front",
    )(x2, pos_lanes, inv_freq_lanes, ln, w_p, qn, kvn, wq_p, wkv)


def _flash_kernel(q_ref, qn_ref, k_ref, v_ref, tri_ref, *rest, tq, dual, lambda_init):
    if dual:
        lam_ref, subln_ref, o_ref, qz_sc, sa_sc, sb_sc, m_sc, acc_sc = rest
    else:
        o_ref, sa_sc, sb_sc, m_sc, acc_sc = rest
    i = pl.program_id(2)
    dv = o_ref.shape[-1]
    rows = m_sc.shape[0]
    half = rows // 2
    wp = tq // 2
    nt = (((1,), (1,)), ((), ()))

    def stack(q_in):
        lo = lax.broadcasted_iota(jnp.int32, q_in.shape, 1) < DA_HEAD_DIM
        zero = jnp.zeros_like(q_in)
        q1 = jnp.where(lo, q_in, zero)
        q2 = jnp.where(lo, zero, q_in)
        return jnp.concatenate([q1[0:wp], q2[0:wp], q1[wp:], q2[wp:]], axis=0)

    if dual:
        qz_sc[...] = stack(q_ref[...])
    m_sc[...] = jnp.full(m_sc.shape, -jnp.inf, F32)
    acc_sc[...] = jnp.zeros(acc_sc.shape, F32)
    q_src = qz_sc if dual else q_ref

    def qk(s_ref, key0, width=tq, r0=0, r1=rows):
        kk = k_ref[pl.ds(pl.multiple_of(key0, wp), width), :]
        s_ref[r0:r1, 0:width] = lax.dot_general(
            q_src[r0:r1, :], kk, nt, preferred_element_type=F32)

    def qk_next():
        qn = stack(qn_ref[...]) if dual else qn_ref[...]
        sa_sc[...] = lax.dot_general(qn, k_ref[0:tq, :], nt, preferred_element_type=F32)

    def chunk_causal(s):
        allowed = tri_ref[...] != 0.0
        groups = [jnp.where(allowed, s[g:g + wp], MASK_VALUE) for g in range(0, s.shape[0], wp)]
        return jnp.concatenate(groups, axis=0)

    def softmax_pv(s_ref, key0, width=tq, r0=0, r1=rows, diag=False):
        s = s_ref[r0:r1, 0:width]
        if diag:
            tail = chunk_causal(s[:, width - wp:])
            s = tail if width == wp else jnp.concatenate([s[:, 0:width - wp], tail], axis=1)
        m_prev = m_sc[r0:r1, :]
        m_new = jnp.maximum(m_prev, jnp.max(s, axis=1, keepdims=True))
        alpha = jnp.exp2(m_prev - m_new)
        p = jnp.exp2((s - jnp.concatenate([m_new] * (width // LANES), axis=1)).astype(BF16))
        vv = v_ref[pl.ds(pl.multiple_of(key0, wp), width), :]
        vv = jnp.concatenate([vv, _ones_column((width, LANES))], axis=1)
        pv = jnp.dot(p, vv, preferred_element_type=F32)
        alpha_w = jnp.concatenate([alpha] * (acc_sc.shape[1] // LANES), axis=1)
        acc_sc[r0:r1, :] = alpha_w * acc_sc[r0:r1, :] + pv
        m_sc[r0:r1, :] = m_new

    def diag_block(s_ref):
        softmax_pv(s_ref, d0, wp, 0, half, diag=True)
        softmax_pv(s_ref, d0, tq, half, rows, diag=True)

    @pl.when(i == 0)
    def _first_block_of_head():
        qk(sa_sc, 0)

    def pair(t, carry):
        qk(sb_sc, (2 * t + 1) * tq)
        softmax_pv(sa_sc, 2 * t * tq)
        qk(sa_sc, (2 * t + 2) * tq)
        softmax_pv(sb_sc, (2 * t + 1) * tq)
        return carry

    lax.fori_loop(0, i // 2, pair, 0)

    d0 = i * tq

    def finalize():
        o = acc_sc[:, 0:dv] / acc_sc[:, dv:dv + 1]
        if dual:
            lp = lam_ref[...]
            lam = (jnp.exp(jnp.sum(lp[0:1] * lp[1:2], axis=1, keepdims=True))
                   - jnp.exp(jnp.sum(lp[2:3] * lp[3:4], axis=1, keepdims=True))
                   + lambda_init)
            o1 = jnp.concatenate([o[0:wp], o[2 * wp:3 * wp]], axis=0)
            o2 = jnp.concatenate([o[wp:2 * wp], o[3 * wp:]], axis=0)
            oa = o1 - lam * o2
            o_ref[...] = (_rms(oa, subln_ref[...]) * (1.0 - lambda_init)).astype(BF16)
        else:
            o_ref[...] = o.astype(BF16)

    @pl.when(i % 2 == 0)
    def _even_tail():
        diag_block(sa_sc)
        qk_next()
        finalize()

    @pl.when(i % 2 == 1)
    def _odd_tail():
        qk(sb_sc, d0, wp, 0, half)
        qk(sb_sc, d0, tq, half, rows)
        softmax_pv(sa_sc, d0 - tq)
        qk_next()
        diag_block(sb_sc)
        finalize()


def _flash(q, k, v, extras, *, blk, dual, lambda_init=0.0):
    bsz, heads, seq, dk = q.shape
    dv = v.shape[-1]
    dvp = dv + LANES
    assert blk % CHUNK == 0 and blk & (blk - 1) == 0 and seq % blk == 0
    rows = 2 * blk if dual else blk
    nq = seq // blk
    qspec = pl.BlockSpec((None, None, blk, dk), lambda b, h, i: (b, h, i, 0))
    qnspec = pl.BlockSpec((None, None, blk, dk),
                          lambda b, h, i: (b, h, jnp.minimum(i + 1, nq - 1), 0))
    kspec = pl.BlockSpec((None, None, seq, dk), lambda b, h, i: (b, h, 0, 0))
    vspec = pl.BlockSpec((None, None, seq, dv), lambda b, h, i: (b, h, 0, 0))
    ospec = pl.BlockSpec((None, blk, dv), lambda b, h, i: (b, i, h))
    wp = blk // 2
    frame_chunk = jnp.arange(wp, dtype=jnp.int32) // CHUNK
    tri = (frame_chunk[None, :] <= frame_chunk[:, None]).astype(F32)
    extras = [tri] + list(extras)
    espec = [pl.BlockSpec(e.shape, lambda b, h, i: (0, 0)) for e in extras]
    scratch = [pltpu.VMEM((rows, blk), F32), pltpu.VMEM((rows, blk), F32),
               pltpu.VMEM((rows, LANES), F32), pltpu.VMEM((rows, dvp), F32)]
    if dual:
        scratch = [pltpu.VMEM((rows, dk), BF16)] + scratch
    kern = functools.partial(_flash_kernel, tq=blk, dual=dual, lambda_init=lambda_init)
    return pl.pallas_call(
        kern,
        grid=(bsz, heads, nq),
        in_specs=[qspec, qnspec, kspec, vspec] + espec,
        out_specs=ospec,
        scratch_shapes=scratch,
        out_shape=jax.ShapeDtypeStruct((bsz, seq, heads * dv), BF16),
        compiler_params=pltpu.CompilerParams(
            dimension_semantics=("parallel", "parallel", "arbitrary"),
            vmem_limit_bytes=VMEM_LIMIT),
        name="flash_da" if dual else "flash_mla",
    )(q, q, k, v, *extras)


def _post_mix_kernel(oa_ref, ob_ref, g_ref, x_ref, wa_ref, wb_ref, wo_ref, ln_ref, o_ref, *, d):
    y_a = jnp.dot(oa_ref[...], wa_ref[...], preferred_element_type=F32)
    y_b = jnp.dot(ob_ref[...], wb_ref[...], preferred_element_type=F32)
    merged = g_ref[:, 0:d].astype(F32) * y_a + g_ref[:, d:2 * d].astype(F32) * y_b
    m = jnp.dot(merged.astype(BF16), wo_ref[...], preferred_element_type=F32)
    o_ref[...] = x_ref[...] + _rms(m, ln_ref[...])


def _post_mix(oa, ob, g, x2, wa, wb, wo, ln, *, tm):
    t, d = x2.shape
    row = lambda w: pl.BlockSpec((tm, w), lambda i: (i, 0))
    return pl.pallas_call(
        functools.partial(_post_mix_kernel, d=d),
        grid=(t // tm,),
        in_specs=[row(d), row(d), row(2 * d), row(d), _resident(wa.shape), _resident(wb.shape),
                  _resident(wo.shape), _resident(ln.shape)],
        out_specs=row(d),
        out_shape=jax.ShapeDtypeStruct((t, d), F32),
        compiler_params=pltpu.CompilerParams(
            dimension_semantics=("parallel",), vmem_limit_bytes=VMEM_LIMIT),
        name="post_mix",
    )(oa, ob, g, x2, wa, wb, wo, ln)


def _ffn_kernel(x_ref, lnp_ref, wg_ref, wu_ref, wd_ref, lno_ref, o_ref):
    x = x_ref[...]
    h = _rms(x, lnp_ref[...]).astype(BF16)
    gate = jnp.dot(h, wg_ref[...], preferred_element_type=F32)
    up = jnp.dot(h, wu_ref[...], preferred_element_type=F32)
    a = (jax.nn.silu(gate) * up).astype(BF16)
    f = jnp.dot(a, wd_ref[...], preferred_element_type=F32)
    o_ref[...] = x + _rms(f, lno_ref[...])


def _ffn(x2, lnp, wg, wu, wd, lno, *, tm):
    t, d = x2.shape
    row = pl.BlockSpec((tm, d), lambda i: (i, 0))
    return pl.pallas_call(
        _ffn_kernel,
        grid=(t // tm,),
        in_specs=[row, _resident(lnp.shape), _resident(wg.shape), _resident(wu.shape),
                  _resident(wd.shape), _resident(lno.shape)],
        out_specs=row,
        out_shape=jax.ShapeDtypeStruct((t, d), F32),
        compiler_params=pltpu.CompilerParams(
            dimension_semantics=("parallel",), vmem_limit_bytes=VMEM_LIMIT),
        name="ffn",
    )(x2, lnp, wg, wu, wd, lno)


def _pack_w_in(w, d):
    lat = MLA_Q_RANK + MLA_KV_RANK + ROPE_DIM
    zm_w = -(-lat // LANES) * LANES
    head, latent, gates = w[:, :3 * d], w[:, 3 * d:3 * d + lat], w[:, 3 * d + lat:]
    latent = jnp.pad(latent, ((0, 0), (0, zm_w - lat)))
    return jnp.concatenate([head, latent, gates], axis=1).astype(BF16), zm_w


def _pack_w_uq(w):
    r = w.shape[0]
    w = w.reshape(r, MLA_HEADS, MLA_QK_DIM)
    w = jnp.pad(w, ((0, 0), (0, 0), (0, MLA_HEAD_PAD - MLA_QK_DIM)))
    return w.reshape(r, MLA_HEADS * MLA_HEAD_PAD).astype(BF16)


def kernel(x, positions, ln_mix_pre, w_in, lambda_q1, lambda_k1, lambda_q2, lambda_k2, da_subln,
           q_a_norm, w_uq, kv_a_norm, w_ukv, w_proj_a, w_proj_b, w_o, ln_mix_post, ln_ffn_pre,
           w_ffn_gate, w_ffn_up, w_ffn_down, ln_ffn_post):
    bsz, seq, d = x.shape
    t = bsz * seq
    depth = w_in.shape[0]
    tm = 512

    half = ROPE_DIM // 2
    inv_freq = 1.0 / (ROPE_THETA ** (jnp.arange(half, dtype=F32) * (2.0 / ROPE_DIM)))
    inv_freq_lanes = jnp.tile(inv_freq, LANES // half).reshape(1, LANES)
    pos_lanes = jnp.broadcast_to(positions.reshape(t, 1), (t, LANES))

    x2 = x.reshape(t, d)
    for l in range(depth):
        lambda_init = 0.8 - 0.6 * math.exp(-0.3 * l)
        w_p, zm_w = _pack_w_in(w_in[l], d)
        q_da, k_da, v_da, gates, q_b, k_b, v_b = _front(
            x2, pos_lanes, inv_freq_lanes, ln_mix_pre[l].reshape(1, d), w_p,
            q_a_norm[l].reshape(1, -1), kv_a_norm[l].reshape(1, -1),
            _pack_w_uq(w_uq[l]), w_ukv[l].astype(BF16),
            bsz=bsz, seq=seq, d=d, zm_w=zm_w, tm=tm,
            qa_scale=DA_HEAD_DIM ** -0.5 * LOG2E, qb_scale=MLA_QK_DIM ** -0.5 * LOG2E)

        lam_p = jnp.stack([lambda_q1[l], lambda_k1[l], lambda_q2[l], lambda_k2[l]]).astype(F32)
        lam_p = jnp.pad(lam_p, ((0, 0), (0, LANES - DA_HEAD_DIM)))
        oa = _flash(q_da, k_da, v_da, [lam_p, da_subln[l].reshape(1, -1)], blk=1024, dual=True,
                    lambda_init=lambda_init)
        ob = _flash(q_b, k_b, v_b, [], blk=1024, dual=False)

        x2 = _post_mix(oa.reshape(t, d), ob.reshape(t, d), gates, x2,
                       w_proj_a[l].astype(BF16), w_proj_b[l].astype(BF16), w_o[l].astype(BF16),
                       ln_mix_post[l].reshape(1, d), tm=tm)
        x2 = _ffn(x2, ln_ffn_pre[l].reshape(1, d), w_ffn_gate[l].astype(BF16),
                  w_ffn_up[l].astype(BF16), w_ffn_down[l].astype(BF16),
                  ln_ffn_post[l].reshape(1, d), tm=tm)
    return x2.reshape(bsz, seq, d)
```

```python
import functools
import math

import jax
import jax.numpy as jnp
from jax import lax
from jax.experimental import pallas as pl
from jax.experimental.pallas import tpu as pltpu

F32 = jnp.float32
BF16 = jnp.bfloat16

LANES = 128
CHUNK = 64
RMS_EPS = 1e-6
ROPE_THETA = 10000.0
ROPE_DIM = 64
MASK_VALUE = -1e30
LOG2E = math.log2(math.e)

DA_HEAD_DIM = 64
MLA_HEADS = 8
MLA_Q_RANK = 384
MLA_KV_RANK = 256
MLA_NOPE_DIM = 128
MLA_V_DIM = 128
MLA_QK_DIM = MLA_NOPE_DIM + ROPE_DIM
MLA_HEAD_PAD = 256

VMEM_LIMIT = 56 * 1024 * 1024


def _rms(x, w, eps=RMS_EPS):
    return x * lax.rsqrt(jnp.mean(x * x, axis=-1, keepdims=True) + eps) * w


def _rope_tile(x, cos, sin_signed, first_half):
    rot = jnp.where(first_half, pltpu.roll(x, 96, 1), pltpu.roll(x, 32, 1))
    return x * cos + rot * sin_signed


def _first_half_mask(shape):
    lane = lax.broadcasted_iota(jnp.int32, shape, 1)
    return (lane & (ROPE_DIM // 2)) == 0


def _ones_column(shape):
    lane = lax.broadcasted_iota(jnp.int32, shape, 1)
    return jnp.where(lane == 0, 1.0, 0.0).astype(BF16)


def _resident(shape):
    return pl.BlockSpec(shape, lambda *_: (0,) * len(shape), pipeline_mode=pl.Buffered(1))


def _front_kernel(x_ref, pos_ref, invf_ref, ln_ref, w_ref, qn_ref, kvn_ref, wq_ref, wkv_ref,
                  qa_ref, ka_ref, va_ref, g_ref, qb_ref, kb_ref, vb_ref,
                  *, d, zm_w, nc, qa_scale, qb_scale):
    h = _rms(x_ref[...], ln_ref[...]).astype(BF16)
    ang = pos_ref[...].astype(F32) * invf_ref[...]
    first = _first_half_mask(ang.shape)
    cos = jnp.cos(ang)
    sin_plain = jnp.sin(ang)
    sin = jnp.where(first, -sin_plain, sin_plain)

    def mm(c0, width):
        return jnp.dot(h, w_ref[:, c0:c0 + width], preferred_element_type=F32)

    for c in range(0, d, nc):
        z = mm(c, nc)
        for g in range(0, nc, LANES):
            r = _rope_tile(z[:, g:g + LANES], cos, sin, first) * qa_scale
            qa_ref[(c + g) // LANES] = r.astype(BF16)
    for c in range(0, d, nc):
        z = mm(d + c, nc)
        for g in range(0, nc, LANES):
            r = _rope_tile(z[:, g:g + LANES], cos, sin, first)
            ka_ref[(c + g) // LANES] = r.astype(BF16)
    for c in range(0, d, nc):
        z = mm(2 * d + c, nc)
        for g in range(0, nc, LANES):
            va_ref[(c + g) // LANES] = z[:, g:g + LANES].astype(BF16)
    for c in range(0, 2 * d, nc):
        g_ref[:, c:c + nc] = jax.nn.sigmoid(mm(3 * d + zm_w + c, nc)).astype(BF16)

    zm = mm(3 * d, zm_w)
    cq = _rms(zm[:, 0:MLA_Q_RANK], qn_ref[...]).astype(BF16)
    ckv = _rms(zm[:, MLA_Q_RANK:MLA_Q_RANK + MLA_KV_RANK], kvn_ref[...]).astype(BF16)
    kr0 = MLA_Q_RANK + MLA_KV_RANK
    k_rope = _rope_tile(zm[:, kr0:kr0 + LANES], cos, sin, first).astype(BF16)
    for hd in range(MLA_HEADS):
        c = hd * MLA_HEAD_PAD
        qh = jnp.dot(cq, wq_ref[:, c:c + MLA_HEAD_PAD], preferred_element_type=F32)
        qb_ref[hd, :, 0:LANES] = (qh[:, :LANES] * qb_scale).astype(BF16)
        qb_ref[hd, :, LANES:2 * LANES] = (
            _rope_tile(qh[:, LANES:], cos, sin, first) * qb_scale).astype(BF16)
        kvh = jnp.dot(ckv, wkv_ref[:, c:c + MLA_HEAD_PAD], preferred_element_type=F32)
        kb_ref[hd, :, 0:LANES] = kvh[:, :LANES].astype(BF16)
        kb_ref[hd, :, LANES:2 * LANES] = k_rope
        vb_ref[hd] = kvh[:, LANES:].astype(BF16)


def _head_major_spec(heads, tm, w, seq):
    per_seq = seq // tm
    return pl.BlockSpec((None, heads, tm, w), lambda i: (i // per_seq, 0, i % per_seq, 0))


def _front(x2, pos_lanes, inv_freq_lanes, ln, w_p, qn, kvn, wq_p, wkv,
           *, bsz, seq, d, zm_w, tm, qa_scale, qb_scale):
    t = x2.shape[0]
    row = lambda w: pl.BlockSpec((tm, w), lambda i: (i, 0))
    da_heads = d // LANES
    hm = lambda heads, w: _head_major_spec(heads, tm, w, seq)
    hm_shape = lambda heads, w: jax.ShapeDtypeStruct((bsz, heads, seq, w), BF16)
    kern = functools.partial(_front_kernel, d=d, zm_w=zm_w, nc=512,
                             qa_scale=qa_scale, qb_scale=qb_scale)
    return pl.pallas_call(
        kern,
        grid=(t // tm,),
        in_specs=[row(d), row(LANES), _resident((1, LANES)), _resident((1, d)),
                  _resident(w_p.shape), _resident(qn.shape), _resident(kvn.shape),
                  _resident(wq_p.shape), _resident(wkv.shape)],
        out_specs=[hm(da_heads, LANES), hm(da_heads, LANES), hm(da_heads, LANES), row(2 * d),
                   hm(MLA_HEADS, MLA_HEAD_PAD), hm(MLA_HEADS, MLA_HEAD_PAD),
                   hm(MLA_HEADS, MLA_V_DIM)],
        out_shape=[hm_shape(da_heads, LANES)] * 3 + [jax.ShapeDtypeStruct((t, 2 * d), BF16)]
        + [hm_shape(MLA_HEADS, MLA_HEAD_PAD)] * 2 + [hm_shape(MLA_HEADS, MLA_V_DIM)],
        compiler_params=pltpu.CompilerParams(
            dimension_semantics=("parallel",), vmem_limit_bytes=VMEM_LIMIT),
        name="front",
    )(x2, pos_lanes, inv_freq_lanes, ln, w_p, qn, kvn, wq_p, wkv)


def _flash_kernel(q_ref, qn_ref, k_ref, v_ref, tri_ref, *rest, tq, nq, dual, lambda_init):
    if dual:
        lam_ref, subln_ref, o_ref, qz_sc, sa_sc, sb_sc, m_sc, acc_sc = rest
    else:
        o_ref, sa_sc, sb_sc, m_sc, acc_sc = rest
    i = pl.program_id(2)
    dv = o_ref.shape[-1]
    rows = m_sc.shape[0]
    half = rows // 2
    wp = tq // 2
    nt = (((1,), (1,)), ((), ()))

    def stack(q_in):
        lo = lax.broadcasted_iota(jnp.int32, q_in.shape, 1) < DA_HEAD_DIM
        zero = jnp.zeros_like(q_in)
        q1 = jnp.where(lo, q_in, zero)
        q2 = jnp.where(lo, zero, q_in)
        return jnp.concatenate([q1[0:wp], q2[0:wp], q1[wp:], q2[wp:]], axis=0)

    if dual:
        qz_sc[...] = stack(q_ref[...])
    m_sc[...] = jnp.full(m_sc.shape, -jnp.inf, F32)
    acc_sc[...] = jnp.zeros(acc_sc.shape, F32)
    q_src = qz_sc if dual else q_ref

    def qk(s_ref, key0, width=tq, r0=0, r1=rows):
        kk = k_ref[pl.ds(pl.multiple_of(key0, wp), width), :]
        s_ref[r0:r1, 0:width] = lax.dot_general(
            q_src[r0:r1, :], kk, nt, preferred_element_type=F32)

    def qk_next():
        qn = stack(qn_ref[...]) if dual else qn_ref[...]
        sa_sc[...] = lax.dot_general(qn, k_ref[0:tq, :], nt, preferred_element_type=F32)

    def chunk_causal(s):
        allowed = tri_ref[...] != 0.0
        groups = [jnp.where(allowed, s[g:g + wp], MASK_VALUE) for g in range(0, s.shape[0], wp)]
        return jnp.concatenate(groups, axis=0)

    def softmax_pv(s_ref, key0, width=tq, r0=0, r1=rows, diag=False):
        s = s_ref[r0:r1, 0:width]
        if diag:
            tail = chunk_causal(s[:, width - wp:])
            s = tail if width == wp else jnp.concatenate([s[:, 0:width - wp], tail], axis=1)
        m_prev = m_sc[r0:r1, :]
        m_new = jnp.maximum(m_prev, jnp.max(s, axis=1, keepdims=True))
        alpha = jnp.exp2(m_prev - m_new)
        p = jnp.exp2((s - jnp.concatenate([m_new] * (width // LANES), axis=1)).astype(BF16))
        vv = v_ref[pl.ds(pl.multiple_of(key0, wp), width), :]
        vv = jnp.concatenate([vv, _ones_column((width, LANES))], axis=1)
        pv = jnp.dot(p, vv, preferred_element_type=F32)
        alpha_w = jnp.concatenate([alpha] * (acc_sc.shape[1] // LANES), axis=1)
        acc_sc[r0:r1, :] = alpha_w * acc_sc[r0:r1, :] + pv
        m_sc[r0:r1, :] = m_new

    def diag_block(s_ref):
        softmax_pv(s_ref, d0, wp, 0, half, diag=True)
        softmax_pv(s_ref, d0, tq, half, rows, diag=True)

    @pl.when(i == 0)
    def _first_block_of_head():
        qk(sa_sc, 0)

    def pair(t, carry):
        qk(sb_sc, (2 * t + 1) * tq)
        softmax_pv(sa_sc, 2 * t * tq)
        qk(sa_sc, (2 * t + 2) * tq)
        softmax_pv(sb_sc, (2 * t + 1) * tq)
        return carry

    lax.fori_loop(0, i // 2, pair, 0)

    d0 = i * tq

    def finalize():
        o = acc_sc[:, 0:dv] / acc_sc[:, dv:dv + 1]
        if dual:
            lp = lam_ref[...]
            lam = (jnp.exp(jnp.sum(lp[0:1] * lp[1:2], axis=1, keepdims=True))
                   - jnp.exp(jnp.sum(lp[2:3] * lp[3:4], axis=1, keepdims=True))
                   + lambda_init)
            o1 = jnp.concatenate([o[0:wp], o[2 * wp:3 * wp]], axis=0)
            o2 = jnp.concatenate([o[wp:2 * wp], o[3 * wp:]], axis=0)
            oa = o1 - lam * o2
            o_ref[...] = (_rms(oa, subln_ref[...]) * (1.0 - lambda_init)).astype(BF16)
        else:
            o_ref[...] = o.astype(BF16)

    def even_tail(prefetch):
        diag_block(sa_sc)
        if prefetch:
            qk_next()
        finalize()

    def odd_tail(prefetch):
        qk(sb_sc, d0, wp, 0, half)
        qk(sb_sc, d0, tq, half, rows)
        softmax_pv(sa_sc, d0 - tq)
        if prefetch:
            qk_next()
        diag_block(sb_sc)
        finalize()

    last = nq - 1
    last_tail, other_tail = (even_tail, odd_tail) if last % 2 == 0 else (odd_tail, even_tail)
    pl.when(i == last)(functools.partial(last_tail, False))
    pl.when((i % 2 == last % 2) & (i != last))(functools.partial(last_tail, True))
    pl.when(i % 2 != last % 2)(functools.partial(other_tail, True))


def _flash(q, k, v, extras, *, blk, dual, lambda_init=0.0):
    bsz, heads, seq, dk = q.shape
    dv = v.shape[-1]
    dvp = dv + LANES
    assert blk % CHUNK == 0 and blk & (blk - 1) == 0 and seq % blk == 0
    rows = 2 * blk if dual else blk
    nq = seq // blk
    qspec = pl.BlockSpec((None, None, blk, dk), lambda b, h, i: (b, h, i, 0))
    qnspec = pl.BlockSpec((None, None, blk, dk),
                          lambda b, h, i: (b, h, jnp.minimum(i + 1, nq - 1), 0))
    kspec = pl.BlockSpec((None, None, seq, dk), lambda b, h, i: (b, h, 0, 0))
    vspec = pl.BlockSpec((None, None, seq, dv), lambda b, h, i: (b, h, 0, 0))
    ospec = pl.BlockSpec((None, blk, dv), lambda b, h, i: (b, i, h))
    wp = blk // 2
    frame_chunk = jnp.arange(wp, dtype=jnp.int32) // CHUNK
    tri = (frame_chunk[None, :] <= frame_chunk[:, None]).astype(F32)
    extras = [tri] + list(extras)
    espec = [pl.BlockSpec(e.shape, lambda b, h, i: (0, 0)) for e in extras]
    scratch = [pltpu.VMEM((rows, blk), F32), pltpu.VMEM((rows, blk), F32),
               pltpu.VMEM((rows, LANES), F32), pltpu.VMEM((rows, dvp), F32)]
    if dual:
        scratch = [pltpu.VMEM((rows, dk), BF16)] + scratch
    kern = functools.partial(_flash_kernel, tq=blk, nq=nq, dual=dual, lambda_init=lambda_init)
    return pl.pallas_call(
        kern,
        grid=(bsz, heads, nq),
        in_specs=[qspec, qnspec, kspec, vspec] + espec,
        out_specs=ospec,
        scratch_shapes=scratch,
        out_shape=jax.ShapeDtypeStruct((bsz, seq, heads * dv), BF16),
        compiler_params=pltpu.CompilerParams(
            dimension_semantics=("parallel", "parallel", "arbitrary"),
            vmem_limit_bytes=VMEM_LIMIT),
        name="flash_da" if dual else "flash_mla",
    )(q, q, k, v, *extras)


def _post_mix_kernel(oa_ref, ob_ref, g_ref, x_ref, wa_ref, wb_ref, wo_ref, ln_ref, o_ref, *, d):
    y_a = jnp.dot(oa_ref[...], wa_ref[...], preferred_element_type=F32)
    y_b = jnp.dot(ob_ref[...], wb_ref[...], preferred_element_type=F32)
    merged = g_ref[:, 0:d].astype(F32) * y_a + g_ref[:, d:2 * d].astype(F32) * y_b
    m = jnp.dot(merged.astype(BF16), wo_ref[...], preferred_element_type=F32)
    o_ref[...] = x_ref[...] + _rms(m, ln_ref[...])


def _post_mix(oa, ob, g, x2, wa, wb, wo, ln, *, tm):
    t, d = x2.shape
    row = lambda w: pl.BlockSpec((tm, w), lambda i: (i, 0))
    return pl.pallas_call(
        functools.partial(_post_mix_kernel, d=d),
        grid=(t // tm,),
        in_specs=[row(d), row(d), row(2 * d), row(d), _resident(wa.shape), _resident(wb.shape),
                  _resident(wo.shape), _resident(ln.shape)],
        out_specs=row(d),
        out_shape=jax.ShapeDtypeStruct((t, d), F32),
        compiler_params=pltpu.CompilerParams(
            dimension_semantics=("parallel",), vmem_limit_bytes=VMEM_LIMIT),
        name="post_mix",
    )(oa, ob, g, x2, wa, wb, wo, ln)


def _ffn_kernel(x_ref, lnp_ref, wg_ref, wu_ref, wd_ref, lno_ref, o_ref):
    x = x_ref[...]
    h = _rms(x, lnp_ref[...]).astype(BF16)
    gate = jnp.dot(h, wg_ref[...], preferred_element_type=F32)
    up = jnp.dot(h, wu_ref[...], preferred_element_type=F32)
    a = (jax.nn.silu(gate) * up).astype(BF16)
    f = jnp.dot(a, wd_ref[...], preferred_element_type=F32)
    o_ref[...] = x + _rms(f, lno_ref[...])


def _ffn(x2, lnp, wg, wu, wd, lno, *, tm):
    t, d = x2.shape
    row = pl.BlockSpec((tm, d), lambda i: (i, 0))
    return pl.pallas_call(
        _ffn_kernel,
        grid=(t // tm,),
        in_specs=[row, _resident(lnp.shape), _resident(wg.shape), _resident(wu.shape),
                  _resident(wd.shape), _resident(lno.shape)],
        out_specs=row,
        out_shape=jax.ShapeDtypeStruct((t, d), F32),
        compiler_params=pltpu.CompilerParams(
            dimension_semantics=("parallel",), vmem_limit_bytes=VMEM_LIMIT),
        name="ffn",
    )(x2, lnp, wg, wu, wd, lno)


def _pack_w_in(w, d):
    lat = MLA_Q_RANK + MLA_KV_RANK + ROPE_DIM
    zm_w = -(-lat // LANES) * LANES
    head, latent, gates = w[:, :3 * d], w[:, 3 * d:3 * d + lat], w[:, 3 * d + lat:]
    latent = jnp.pad(latent, ((0, 0), (0, zm_w - lat)))
    return jnp.concatenate([head, latent, gates], axis=1).astype(BF16), zm_w


def _pack_w_uq(w):
    r = w.shape[0]
    w = w.reshape(r, MLA_HEADS, MLA_QK_DIM)
    w = jnp.pad(w, ((0, 0), (0, 0), (0, MLA_HEAD_PAD - MLA_QK_DIM)))
    return w.reshape(r, MLA_HEADS * MLA_HEAD_PAD).astype(BF16)


def kernel(x, positions, ln_mix_pre, w_in, lambda_q1, lambda_k1, lambda_q2, lambda_k2, da_subln,
           q_a_norm, w_uq, kv_a_norm, w_ukv, w_proj_a, w_proj_b, w_o, ln_mix_post, ln_ffn_pre,
           w_ffn_gate, w_ffn_up, w_ffn_down, ln_ffn_post):
    bsz, seq, d = x.shape
    t = bsz * seq
    depth = w_in.shape[0]
    tm = 512

    half = ROPE_DIM // 2
    inv_freq = 1.0 / (ROPE_THETA ** (jnp.arange(half, dtype=F32) * (2.0 / ROPE_DIM)))
    inv_freq_lanes = jnp.tile(inv_freq, LANES // half).reshape(1, LANES)
    pos_lanes = jnp.broadcast_to(positions.reshape(t, 1), (t, LANES))

    x2 = x.reshape(t, d)
    for l in range(depth):
        lambda_init = 0.8 - 0.6 * math.exp(-0.3 * l)
        w_p, zm_w = _pack_w_in(w_in[l], d)
        q_da, k_da, v_da, gates, q_b, k_b, v_b = _front(
            x2, pos_lanes, inv_freq_lanes, ln_mix_pre[l].reshape(1, d), w_p,
            q_a_norm[l].reshape(1, -1), kv_a_norm[l].reshape(1, -1),
            _pack_w_uq(w_uq[l]), w_ukv[l].astype(BF16),
            bsz=bsz, seq=seq, d=d, zm_w=zm_w, tm=tm,
            qa_scale=DA_HEAD_DIM ** -0.5 * LOG2E, qb_scale=MLA_QK_DIM ** -0.5 * LOG2E)

        lam_p = jnp.stack([lambda_q1[l], lambda_k1[l], lambda_q2[l], lambda_k2[l]]).astype(F32)
        lam_p = jnp.pad(lam_p, ((0, 0), (0, LANES - DA_HEAD_DIM)))
        oa = _flash(q_da, k_da, v_da, [lam_p, da_subln[l].reshape(1, -1)], blk=1024, dual=True,
                    lambda_init=lambda_init)
        ob = _flash(q_b, k_b, v_b, [], blk=1024, dual=False)

        x2 = _post_mix(oa.reshape(t, d), ob.reshape(t, d), gates, x2,
                       w_proj_a[l].astype(BF16), w_proj_b[l].astype(BF16), w_o[l].astype(BF16),
                       ln_mix_post[l].reshape(1, d), tm=tm)
        x2 = _ffn(x2, ln_ffn_pre[l].reshape(1, d), w_ffn_gate[l].astype(BF16),
                  w_ffn_up[l].astype(BF16), w_ffn_down[l].astype(BF16),
                  ln_ffn_post[l].reshape(1, d), tm=tm)
    return x2.reshape(bsz, seq, d)
```

```python
import functools
import math

import jax
import jax.numpy as jnp
from jax import lax
from jax.experimental import pallas as pl
from jax.experimental.pallas import tpu as pltpu

F32 = jnp.float32
BF16 = jnp.bfloat16

LANES = 128
CHUNK = 64
RMS_EPS = 1e-6
ROPE_THETA = 10000.0
ROPE_DIM = 64
MASK_VALUE = -1e30
LOG2E = math.log2(math.e)

DA_HEAD_DIM = 64
MLA_HEADS = 8
MLA_Q_RANK = 384
MLA_KV_RANK = 256
MLA_NOPE_DIM = 128
MLA_V_DIM = 128
MLA_QK_DIM = MLA_NOPE_DIM + ROPE_DIM
MLA_HEAD_PAD = 256

VMEM_LIMIT = 56 * 1024 * 1024


def _rms(x, w, eps=RMS_EPS):
    return x * lax.rsqrt(jnp.mean(x * x, axis=-1, keepdims=True) + eps) * w


def _rope_tile(x, cos, sin_signed, first_half):
    rot = jnp.where(first_half, pltpu.roll(x, 96, 1), pltpu.roll(x, 32, 1))
    return x * cos + rot * sin_signed


def _first_half_mask(shape):
    lane = lax.broadcasted_iota(jnp.int32, shape, 1)
    return (lane & (ROPE_DIM // 2)) == 0


def _ones_column(shape):
    lane = lax.broadcasted_iota(jnp.int32, shape, 1)
    return jnp.where(lane == 0, 1.0, 0.0).astype(BF16)


def _resident(shape):
    return pl.BlockSpec(shape, lambda *_: (0,) * len(shape), pipeline_mode=pl.Buffered(1))


def _front_kernel(x_ref, pos_ref, invf_ref, ln_ref, wqkv_ref, wlat_ref, wgate_ref,
                  qn_ref, kvn_ref, wq_ref, wkv_ref,
                  qa_ref, ka_ref, va_ref, g_ref, qb_ref, kb_ref, vb_ref,
                  *, d, nc, qa_scale, qb_scale):
    h = _rms(x_ref[...], ln_ref[...]).astype(BF16)
    ang = pos_ref[...].astype(F32) * invf_ref[...]
    first = _first_half_mask(ang.shape)
    cos = jnp.cos(ang)
    sin_plain = jnp.sin(ang)
    sin = jnp.where(first, -sin_plain, sin_plain)

    def mm(c0, width, w_ref=wqkv_ref):
        return jnp.dot(h, w_ref[:, c0:c0 + width], preferred_element_type=F32)

    for c in range(0, d, nc):
        z = mm(c, nc)
        for g in range(0, nc, LANES):
            r = _rope_tile(z[:, g:g + LANES], cos, sin, first) * qa_scale
            qa_ref[(c + g) // LANES] = r.astype(BF16)
    for c in range(0, d, nc):
        z = mm(d + c, nc)
        for g in range(0, nc, LANES):
            r = _rope_tile(z[:, g:g + LANES], cos, sin, first)
            ka_ref[(c + g) // LANES] = r.astype(BF16)
    for c in range(0, d, nc):
        z = mm(2 * d + c, nc)
        for g in range(0, nc, LANES):
            va_ref[(c + g) // LANES] = z[:, g:g + LANES].astype(BF16)
    for c in range(0, 2 * d, nc):
        g_ref[:, c:c + nc] = jax.nn.sigmoid(mm(c, nc, wgate_ref)).astype(BF16)

    zm = mm(0, wlat_ref.shape[1], wlat_ref)
    cq = _rms(zm[:, 0:MLA_Q_RANK], qn_ref[...]).astype(BF16)
    ckv = _rms(zm[:, MLA_Q_RANK:MLA_Q_RANK + MLA_KV_RANK], kvn_ref[...]).astype(BF16)
    kr0 = MLA_Q_RANK + MLA_KV_RANK
    k_rope = _rope_tile(zm[:, kr0:kr0 + LANES], cos, sin, first).astype(BF16)
    for hd in range(MLA_HEADS):
        c = hd * MLA_HEAD_PAD
        qh = jnp.dot(cq, wq_ref[:, c:c + MLA_HEAD_PAD], preferred_element_type=F32)
        qb_ref[hd, :, 0:LANES] = (qh[:, :LANES] * qb_scale).astype(BF16)
        qb_ref[hd, :, LANES:2 * LANES] = (
            _rope_tile(qh[:, LANES:], cos, sin, first) * qb_scale).astype(BF16)
        kvh = jnp.dot(ckv, wkv_ref[:, c:c + MLA_HEAD_PAD], preferred_element_type=F32)
        kb_ref[hd, :, 0:LANES] = kvh[:, :LANES].astype(BF16)
        kb_ref[hd, :, LANES:2 * LANES] = k_rope
        vb_ref[hd] = kvh[:, LANES:].astype(BF16)


def _head_major_spec(heads, tm, w, seq):
    per_seq = seq // tm
    return pl.BlockSpec((None, heads, tm, w), lambda i: (i // per_seq, 0, i % per_seq, 0))


def _front(x2, pos_lanes, inv_freq_lanes, ln, w_parts, qn, kvn, wq_p, wkv,
           *, bsz, seq, d, tm, qa_scale, qb_scale):
    t = x2.shape[0]
    row = lambda w: pl.BlockSpec((tm, w), lambda i: (i, 0))
    da_heads = d // LANES
    hm = lambda heads, w: _head_major_spec(heads, tm, w, seq)
    hm_shape = lambda heads, w: jax.ShapeDtypeStruct((bsz, heads, seq, w), BF16)
    kern = functools.partial(_front_kernel, d=d, nc=512, qa_scale=qa_scale, qb_scale=qb_scale)
    return pl.pallas_call(
        kern,
        grid=(t // tm,),
        in_specs=[row(d), row(LANES), _resident((1, LANES)), _resident((1, d))]
        + [_resident(w.shape) for w in w_parts]
        + [_resident(qn.shape), _resident(kvn.shape), _resident(wq_p.shape),
           _resident(wkv.shape)],
        out_specs=[hm(da_heads, LANES), hm(da_heads, LANES), hm(da_heads, LANES), row(2 * d),
                   hm(MLA_HEADS, MLA_HEAD_PAD), hm(MLA_HEADS, MLA_HEAD_PAD),
                   hm(MLA_HEADS, MLA_V_DIM)],
        out_shape=[hm_shape(da_heads, LANES)] * 3 + [jax.ShapeDtypeStruct((t, 2 * d), BF16)]
        + [hm_shape(MLA_HEADS, MLA_HEAD_PAD)] * 2 + [hm_shape(MLA_HEADS, MLA_V_DIM)],
        compiler_params=pltpu.CompilerParams(
            dimension_semantics=("parallel",), vmem_limit_bytes=VMEM_LIMIT),
        name="front",
    )(x2, pos_lanes, inv_freq_lanes, ln, *w_parts, qn, kvn, wq_p, wkv)


def _flash_kernel(q_ref, qn_ref, k_ref, v_ref, tri_ref, *rest, tq, nq, dual, lambda_init):
    if dual:
        lam_ref, subln_ref, o_ref, qz_sc, sa_sc, sb_sc, m_sc, acc_sc = rest
    else:
        o_ref, sa_sc, sb_sc, m_sc, acc_sc = rest
    i = pl.program_id(2)
    dv = o_ref.shape[-1]
    rows = m_sc.shape[0]
    half = rows // 2
    wp = tq // 2
    nt = (((1,), (1,)), ((), ()))

    def stack(q_in):
        lo = lax.broadcasted_iota(jnp.int32, q_in.shape, 1) < DA_HEAD_DIM
        zero = jnp.zeros_like(q_in)
        q1 = jnp.where(lo, q_in, zero)
        q2 = jnp.where(lo, zero, q_in)
        return jnp.concatenate([q1[0:wp], q2[0:wp], q1[wp:], q2[wp:]], axis=0)

    if dual:
        qz_sc[...] = stack(q_ref[...])
    m_sc[...] = jnp.full(m_sc.shape, -jnp.inf, F32)
    acc_sc[...] = jnp.zeros(acc_sc.shape, F32)
    q_src = qz_sc if dual else q_ref

    def qk(s_ref, key0, width=tq, r0=0, r1=rows):
        kk = k_ref[pl.ds(pl.multiple_of(key0, wp), width), :]
        s_ref[r0:r1, 0:width] = lax.dot_general(
            q_src[r0:r1, :], kk, nt, preferred_element_type=F32)

    def qk_next():
        qn = stack(qn_ref[...]) if dual else qn_ref[...]
        sa_sc[...] = lax.dot_general(qn, k_ref[0:tq, :], nt, preferred_element_type=F32)

    def chunk_causal(s):
        allowed = tri_ref[...] != 0.0
        groups = [jnp.where(allowed, s[g:g + wp], MASK_VALUE) for g in range(0, s.shape[0], wp)]
        return jnp.concatenate(groups, axis=0)

    def softmax_pv(s_ref, key0, width=tq, r0=0, r1=rows, diag=False):
        s = s_ref[r0:r1, 0:width]
        if diag:
            tail = chunk_causal(s[:, width - wp:])
            s = tail if width == wp else jnp.concatenate([s[:, 0:width - wp], tail], axis=1)
        m_prev = m_sc[r0:r1, :]
        m_new = jnp.maximum(m_prev, jnp.max(s, axis=1, keepdims=True))
        alpha = jnp.exp2(m_prev - m_new)
        p = jnp.exp2((s - jnp.concatenate([m_new] * (width // LANES), axis=1)).astype(BF16))
        vv = v_ref[pl.ds(pl.multiple_of(key0, wp), width), :]
        vv = jnp.concatenate([vv, _ones_column((width, LANES))], axis=1)
        pv = jnp.dot(p, vv, preferred_element_type=F32)
        alpha_w = jnp.concatenate([alpha] * (acc_sc.shape[1] // LANES), axis=1)
        acc_sc[r0:r1, :] = alpha_w * acc_sc[r0:r1, :] + pv
        m_sc[r0:r1, :] = m_new

    def diag_block(s_ref):
        softmax_pv(s_ref, d0, wp, 0, half, diag=True)
        softmax_pv(s_ref, d0, tq, half, rows, diag=True)

    @pl.when(i == 0)
    def _first_block_of_head():
        qk(sa_sc, 0)

    def pair(t, carry):
        qk(sb_sc, (2 * t + 1) * tq)
        softmax_pv(sa_sc, 2 * t * tq)
        qk(sa_sc, (2 * t + 2) * tq)
        softmax_pv(sb_sc, (2 * t + 1) * tq)
        return carry

    lax.fori_loop(0, i // 2, pair, 0)

    d0 = i * tq

    def finalize():
        o = acc_sc[:, 0:dv] / acc_sc[:, dv:dv + 1]
        if dual:
            lp = lam_ref[...]
            lam = (jnp.exp(jnp.sum(lp[0:1] * lp[1:2], axis=1, keepdims=True))
                   - jnp.exp(jnp.sum(lp[2:3] * lp[3:4], axis=1, keepdims=True))
                   + lambda_init)
            o1 = jnp.concatenate([o[0:wp], o[2 * wp:3 * wp]], axis=0)
            o2 = jnp.concatenate([o[wp:2 * wp], o[3 * wp:]], axis=0)
            oa = o1 - lam * o2
            o_ref[...] = (_rms(oa, subln_ref[...]) * (1.0 - lambda_init)).astype(BF16)
        else:
            o_ref[...] = o.astype(BF16)

    def even_tail(prefetch):
        diag_block(sa_sc)
        if prefetch:
            qk_next()
        finalize()

    def odd_tail(prefetch):
        qk(sb_sc, d0, wp, 0, half)
        qk(sb_sc, d0, tq, half, rows)
        softmax_pv(sa_sc, d0 - tq)
        if prefetch:
            qk_next()
        diag_block(sb_sc)
        finalize()

    last = nq - 1
    last_tail, other_tail = (even_tail, odd_tail) if last % 2 == 0 else (odd_tail, even_tail)
    pl.when(i == last)(functools.partial(last_tail, False))
    pl.when((i % 2 == last % 2) & (i != last))(functools.partial(last_tail, True))
    pl.when(i % 2 != last % 2)(functools.partial(other_tail, True))


def _flash(q, k, v, extras, *, blk, dual, lambda_init=0.0):
    bsz, heads, seq, dk = q.shape
    dv = v.shape[-1]
    dvp = dv + LANES
    assert blk % CHUNK == 0 and blk & (blk - 1) == 0 and seq % blk == 0
    rows = 2 * blk if dual else blk
    nq = seq // blk
    qspec = pl.BlockSpec((None, None, blk, dk), lambda b, h, i: (b, h, i, 0))
    qnspec = pl.BlockSpec((None, None, blk, dk),
                          lambda b, h, i: (b, h, jnp.minimum(i + 1, nq - 1), 0))
    kspec = pl.BlockSpec((None, None, seq, dk), lambda b, h, i: (b, h, 0, 0))
    vspec = pl.BlockSpec((None, None, seq, dv), lambda b, h, i: (b, h, 0, 0))
    ospec = pl.BlockSpec((None, blk, dv), lambda b, h, i: (b, i, h))
    wp = blk // 2
    frame_chunk = jnp.arange(wp, dtype=jnp.int32) // CHUNK
    tri = (frame_chunk[None, :] <= frame_chunk[:, None]).astype(F32)
    extras = [tri] + list(extras)
    espec = [pl.BlockSpec(e.shape, lambda b, h, i: (0, 0)) for e in extras]
    scratch = [pltpu.VMEM((rows, blk), F32), pltpu.VMEM((rows, blk), F32),
               pltpu.VMEM((rows, LANES), F32), pltpu.VMEM((rows, dvp), F32)]
    if dual:
        scratch = [pltpu.VMEM((rows, dk), BF16)] + scratch
    kern = functools.partial(_flash_kernel, tq=blk, nq=nq, dual=dual, lambda_init=lambda_init)
    return pl.pallas_call(
        kern,
        grid=(bsz, heads, nq),
        in_specs=[qspec, qnspec, kspec, vspec] + espec,
        out_specs=ospec,
        scratch_shapes=scratch,
        out_shape=jax.ShapeDtypeStruct((bsz, seq, heads * dv), BF16),
        compiler_params=pltpu.CompilerParams(
            dimension_semantics=("parallel", "parallel", "arbitrary"),
            vmem_limit_bytes=VMEM_LIMIT),
        name="flash_da" if dual else "flash_mla",
    )(q, q, k, v, *extras)


def _back_kernel(oa_ref, ob_ref, g_ref, x_ref, wa_ref, wb_ref, wo_ref, lnm_ref,
                 lnp_ref, wg_ref, wu_ref, wd_ref, lno_ref, o_ref, *, d, ff_chunk):
    y_a = jnp.dot(oa_ref[...], wa_ref[...], preferred_element_type=F32)
    y_b = jnp.dot(ob_ref[...], wb_ref[...], preferred_element_type=F32)
    merged = g_ref[:, 0:d].astype(F32) * y_a + g_ref[:, d:2 * d].astype(F32) * y_b
    m = jnp.dot(merged.astype(BF16), wo_ref[...], preferred_element_type=F32)
    x1 = x_ref[...] + _rms(m, lnm_ref[...])

    h = _rms(x1, lnp_ref[...]).astype(BF16)
    f = None
    d_ff = wg_ref.shape[1]
    for c in range(0, d_ff, ff_chunk):
        w = min(ff_chunk, d_ff - c)
        gate = jnp.dot(h, wg_ref[:, c:c + w], preferred_element_type=F32)
        up = jnp.dot(h, wu_ref[:, c:c + w], preferred_element_type=F32)
        a = (jax.nn.silu(gate) * up).astype(BF16)
        part = jnp.dot(a, wd_ref[c:c + w, :], preferred_element_type=F32)
        f = part if f is None else f + part
    o_ref[...] = x1 + _rms(f, lno_ref[...])


def _back(oa, ob, g, x2, wa, wb, wo, lnm, lnp, wg, wu, wd, lno, *, tm):
    t, d = x2.shape
    ff_chunk = 1024
    row = lambda w: pl.BlockSpec((tm, w), lambda i: (i, 0))
    weights = [wa, wb, wo, lnm, lnp, wg, wu, wd, lno]
    return pl.pallas_call(
        functools.partial(_back_kernel, d=d, ff_chunk=ff_chunk),
        grid=(t // tm,),
        in_specs=[row(d), row(d), row(2 * d), row(d)] + [_resident(w.shape) for w in weights],
        out_specs=row(d),
        out_shape=jax.ShapeDtypeStruct((t, d), F32),
        compiler_params=pltpu.CompilerParams(
            dimension_semantics=("parallel",), vmem_limit_bytes=VMEM_LIMIT),
        name="back",
    )(oa, ob, g, x2, *weights)


def _split_w_in(w, d):
    lat = MLA_Q_RANK + MLA_KV_RANK + ROPE_DIM
    zm_w = -(-lat // LANES) * LANES
    latent = jnp.pad(w[:, 3 * d:3 * d + lat].astype(BF16), ((0, 0), (0, zm_w - lat)))
    return [w[:, :3 * d].astype(BF16), latent, w[:, 3 * d + lat:].astype(BF16)]


def _pack_w_uq(w):
    r = w.shape[0]
    w = w.reshape(r, MLA_HEADS, MLA_QK_DIM)
    w = jnp.pad(w, ((0, 0), (0, 0), (0, MLA_HEAD_PAD - MLA_QK_DIM)))
    return w.reshape(r, MLA_HEADS * MLA_HEAD_PAD).astype(BF16)


def kernel(x, positions, ln_mix_pre, w_in, lambda_q1, lambda_k1, lambda_q2, lambda_k2, da_subln,
           q_a_norm, w_uq, kv_a_norm, w_ukv, w_proj_a, w_proj_b, w_o, ln_mix_post, ln_ffn_pre,
           w_ffn_gate, w_ffn_up, w_ffn_down, ln_ffn_post):
    bsz, seq, d = x.shape
    t = bsz * seq
    depth = w_in.shape[0]
    tm = 512

    half = ROPE_DIM // 2
    inv_freq = 1.0 / (ROPE_THETA ** (jnp.arange(half, dtype=F32) * (2.0 / ROPE_DIM)))
    inv_freq_lanes = jnp.tile(inv_freq, LANES // half).reshape(1, LANES)
    pos_lanes = jnp.broadcast_to(positions.reshape(t, 1), (t, LANES))

    x2 = x.reshape(t, d)
    for l in range(depth):
        lambda_init = 0.8 - 0.6 * math.exp(-0.3 * l)
        q_da, k_da, v_da, gates, q_b, k_b, v_b = _front(
            x2, pos_lanes, inv_freq_lanes, ln_mix_pre[l].reshape(1, d), _split_w_in(w_in[l], d),
            q_a_norm[l].reshape(1, -1), kv_a_norm[l].reshape(1, -1),
            _pack_w_uq(w_uq[l]), w_ukv[l].astype(BF16),
            bsz=bsz, seq=seq, d=d, tm=tm,
            qa_scale=DA_HEAD_DIM ** -0.5 * LOG2E, qb_scale=MLA_QK_DIM ** -0.5 * LOG2E)

        lam_p = jnp.stack([lambda_q1[l], lambda_k1[l], lambda_q2[l], lambda_k2[l]]).astype(F32)
        lam_p = jnp.pad(lam_p, ((0, 0), (0, LANES - DA_HEAD_DIM)))
        oa = _flash(q_da, k_da, v_da, [lam_p, da_subln[l].reshape(1, -1)], blk=1024, dual=True,
                    lambda_init=lambda_init)
        ob = _flash(q_b, k_b, v_b, [], blk=1024, dual=False)

        x2 = _back(oa.reshape(t, d), ob.reshape(t, d), gates, x2,
                   w_proj_a[l].astype(BF16), w_proj_b[l].astype(BF16), w_o[l].astype(BF16),
                   ln_mix_post[l].reshape(1, d), ln_ffn_pre[l].reshape(1, d),
                   w_ffn_gate[l].astype(BF16), w_ffn_up[l].astype(BF16),
                   w_ffn_down[l].astype(BF16), ln_ffn_post[l].reshape(1, d), tm=tm)
    return x2.reshape(bsz, seq, d)
```

```python
import functools
import math

import jax
import jax.numpy as jnp
from jax import lax
from jax.experimental import pallas as pl
from jax.experimental.pallas import tpu as pltpu

F32 = jnp.float32
BF16 = jnp.bfloat16

LANES = 128
CHUNK = 64
RMS_EPS = 1e-6
ROPE_THETA = 10000.0
ROPE_DIM = 64
MASK_VALUE = -1e30
LOG2E = math.log2(math.e)

DA_HEAD_DIM = 64
MLA_HEADS = 8
MLA_Q_RANK = 384
MLA_KV_RANK = 256
MLA_NOPE_DIM = 128
MLA_V_DIM = 128
MLA_QK_DIM = MLA_NOPE_DIM + ROPE_DIM
MLA_HEAD_PAD = 256

VMEM_LIMIT = 56 * 1024 * 1024


def _rms(x, w, eps=RMS_EPS):
    return x * lax.rsqrt(jnp.mean(x * x, axis=-1, keepdims=True) + eps) * w


def _rope_tile(x, cos, sin_signed, first_half):
    rot = jnp.where(first_half, pltpu.roll(x, 96, 1), pltpu.roll(x, 32, 1))
    return x * cos + rot * sin_signed


def _first_half_mask(shape):
    lane = lax.broadcasted_iota(jnp.int32, shape, 1)
    return (lane & (ROPE_DIM // 2)) == 0


def _ones_column(shape):
    lane = lax.broadcasted_iota(jnp.int32, shape, 1)
    return jnp.where(lane == 0, 1.0, 0.0).astype(BF16)


def _resident(shape):
    return pl.BlockSpec(shape, lambda *_: (0,) * len(shape), pipeline_mode=pl.Buffered(1))


def _front_kernel(x_ref, pos_ref, invf_ref, ln_ref, wqkv_ref, wlat_ref, wgate_ref,
                  qn_ref, kvn_ref, wq_ref, wkv_ref,
                  qa_ref, ka_ref, va_ref, g_ref, qb_ref, kb_ref, vb_ref,
                  *, d, nc, qa_scale, qb_scale):
    h = _rms(x_ref[...], ln_ref[...]).astype(BF16)
    ang = pos_ref[...].astype(F32) * invf_ref[...]
    first = _first_half_mask(ang.shape)
    cos = jnp.cos(ang)
    sin_plain = jnp.sin(ang)
    sin = jnp.where(first, -sin_plain, sin_plain)

    def mm(c0, width, w_ref=wqkv_ref):
        return jnp.dot(h, w_ref[:, c0:c0 + width], preferred_element_type=F32)

    for c in range(0, d, nc):
        z = mm(c, nc)
        for g in range(0, nc, LANES):
            r = _rope_tile(z[:, g:g + LANES], cos, sin, first) * qa_scale
            qa_ref[(c + g) // LANES] = r.astype(BF16)
    for c in range(0, d, nc):
        z = mm(d + c, nc)
        for g in range(0, nc, LANES):
            r = _rope_tile(z[:, g:g + LANES], cos, sin, first)
            ka_ref[(c + g) // LANES] = r.astype(BF16)
    for c in range(0, d, nc):
        z = mm(2 * d + c, nc)
        for g in range(0, nc, LANES):
            va_ref[(c + g) // LANES] = z[:, g:g + LANES].astype(BF16)
    for c in range(0, 2 * d, nc):
        g_ref[:, c:c + nc] = jax.nn.sigmoid(mm(c, nc, wgate_ref)).astype(BF16)

    zm = mm(0, wlat_ref.shape[1], wlat_ref)
    cq = _rms(zm[:, 0:MLA_Q_RANK], qn_ref[...]).astype(BF16)
    ckv = _rms(zm[:, MLA_Q_RANK:MLA_Q_RANK + MLA_KV_RANK], kvn_ref[...]).astype(BF16)
    kr0 = MLA_Q_RANK + MLA_KV_RANK
    k_rope = _rope_tile(zm[:, kr0:kr0 + LANES], cos, sin, first).astype(BF16)
    for hd in range(MLA_HEADS):
        c = hd * MLA_HEAD_PAD
        qh = jnp.dot(cq, wq_ref[:, c:c + MLA_HEAD_PAD], preferred_element_type=F32)
        qb_ref[hd, :, 0:LANES] = (qh[:, :LANES] * qb_scale).astype(BF16)
        qb_ref[hd, :, LANES:2 * LANES] = (
            _rope_tile(qh[:, LANES:], cos, sin, first) * qb_scale).astype(BF16)
        kvh = jnp.dot(ckv, wkv_ref[:, c:c + MLA_HEAD_PAD], preferred_element_type=F32)
        kb_ref[hd, :, 0:LANES] = kvh[:, :LANES].astype(BF16)
        kb_ref[hd, :, LANES:2 * LANES] = k_rope
        vb_ref[hd] = kvh[:, LANES:].astype(BF16)


def _head_major_spec(heads, tm, w, seq):
    per_seq = seq // tm
    return pl.BlockSpec((None, heads, tm, w), lambda i: (i // per_seq, 0, i % per_seq, 0))


def _front(x2, pos_lanes, inv_freq_lanes, ln, w_parts, qn, kvn, wq_p, wkv,
           *, bsz, seq, d, tm, qa_scale, qb_scale):
    t = x2.shape[0]
    row = lambda w: pl.BlockSpec((tm, w), lambda i: (i, 0))
    da_heads = d // LANES
    hm = lambda heads, w: _head_major_spec(heads, tm, w, seq)
    hm_shape = lambda heads, w: jax.ShapeDtypeStruct((bsz, heads, seq, w), BF16)
    kern = functools.partial(_front_kernel, d=d, nc=512, qa_scale=qa_scale, qb_scale=qb_scale)
    return pl.pallas_call(
        kern,
        grid=(t // tm,),
        in_specs=[row(d), row(LANES), _resident((1, LANES)), _resident((1, d))]
        + [_resident(w.shape) for w in w_parts]
        + [_resident(qn.shape), _resident(kvn.shape), _resident(wq_p.shape),
           _resident(wkv.shape)],
        out_specs=[hm(da_heads, LANES), hm(da_heads, LANES), hm(da_heads, LANES), row(2 * d),
                   hm(MLA_HEADS, MLA_HEAD_PAD), hm(MLA_HEADS, MLA_HEAD_PAD),
                   hm(MLA_HEADS, MLA_V_DIM)],
        out_shape=[hm_shape(da_heads, LANES)] * 3 + [jax.ShapeDtypeStruct((t, 2 * d), BF16)]
        + [hm_shape(MLA_HEADS, MLA_HEAD_PAD)] * 2 + [hm_shape(MLA_HEADS, MLA_V_DIM)],
        compiler_params=pltpu.CompilerParams(
            dimension_semantics=("parallel",), vmem_limit_bytes=VMEM_LIMIT),
        name="front",
    )(x2, pos_lanes, inv_freq_lanes, ln, *w_parts, qn, kvn, wq_p, wkv)


def _flash_kernel(q_ref, qn_ref, k_ref, v_ref, tri_ref, *rest, tq, nq, dual, lambda_init):
    if dual:
        lam_ref, subln_ref, o_ref, qz_sc, sa_sc, sb_sc, m_sc, acc_sc = rest
    else:
        o_ref, sa_sc, sb_sc, m_sc, acc_sc = rest
    i = pl.program_id(2)
    dv = o_ref.shape[-1]
    rows = m_sc.shape[0]
    half = rows // 2
    wp = tq // 2
    nt = (((1,), (1,)), ((), ()))

    def stack(q_in):
        lo = lax.broadcasted_iota(jnp.int32, q_in.shape, 1) < DA_HEAD_DIM
        zero = jnp.zeros_like(q_in)
        q1 = jnp.where(lo, q_in, zero)
        q2 = jnp.where(lo, zero, q_in)
        return jnp.concatenate([q1[0:wp], q2[0:wp], q1[wp:], q2[wp:]], axis=0)

    if dual:
        qz_sc[...] = stack(q_ref[...])
    m_sc[...] = jnp.full(m_sc.shape, -jnp.inf, F32)
    acc_sc[...] = jnp.zeros(acc_sc.shape, F32)
    q_src = qz_sc if dual else q_ref

    def qk(s_ref, key0, width=tq, r0=0, r1=rows):
        kk = k_ref[pl.ds(pl.multiple_of(key0, wp), width), :]
        s_ref[r0:r1, 0:width] = lax.dot_general(
            q_src[r0:r1, :], kk, nt, preferred_element_type=F32)

    def qk_next():
        qn = stack(qn_ref[...]) if dual else qn_ref[...]
        sa_sc[...] = lax.dot_general(qn, k_ref[0:tq, :], nt, preferred_element_type=F32)

    def chunk_causal(s):
        allowed = tri_ref[...] != 0.0
        groups = [jnp.where(allowed, s[g:g + wp], MASK_VALUE) for g in range(0, s.shape[0], wp)]
        return jnp.concatenate(groups, axis=0)

    def softmax_pv(s_ref, key0, width=tq, r0=0, r1=rows, diag=False):
        s = s_ref[r0:r1, 0:width]
        if diag:
            tail = chunk_causal(s[:, width - wp:])
            s = tail if width == wp else jnp.concatenate([s[:, 0:width - wp], tail], axis=1)
        m_prev = m_sc[r0:r1, :]
        m_new = jnp.maximum(m_prev, jnp.max(s, axis=1, keepdims=True))
        alpha = jnp.exp2(m_prev - m_new)
        p = jnp.exp2((s - jnp.concatenate([m_new] * (width // LANES), axis=1)).astype(BF16))
        vv = v_ref[pl.ds(pl.multiple_of(key0, wp), width), :]
        vv = jnp.concatenate([vv, _ones_column((width, LANES))], axis=1)
        pv = jnp.dot(p, vv, preferred_element_type=F32)
        alpha_w = jnp.concatenate([alpha] * (acc_sc.shape[1] // LANES), axis=1)
        acc_sc[r0:r1, :] = alpha_w * acc_sc[r0:r1, :] + pv
        m_sc[r0:r1, :] = m_new

    def diag_block(s_ref):
        softmax_pv(s_ref, d0, wp, 0, half, diag=True)
        softmax_pv(s_ref, d0, tq, half, rows, diag=True)

    @pl.when(i == 0)
    def _first_block_of_head():
        qk(sa_sc, 0)

    def pair(t, carry):
        qk(sb_sc, (2 * t + 1) * tq)
        softmax_pv(sa_sc, 2 * t * tq)
        qk(sa_sc, (2 * t + 2) * tq)
        softmax_pv(sb_sc, (2 * t + 1) * tq)
        return carry

    lax.fori_loop(0, i // 2, pair, 0)

    d0 = i * tq

    def finalize():
        o = acc_sc[:, 0:dv] / acc_sc[:, dv:dv + 1]
        if dual:
            lp = lam_ref[...]
            lam = (jnp.exp(jnp.sum(lp[0:1] * lp[1:2], axis=1, keepdims=True))
                   - jnp.exp(jnp.sum(lp[2:3] * lp[3:4], axis=1, keepdims=True))
                   + lambda_init)
            o1 = jnp.concatenate([o[0:wp], o[2 * wp:3 * wp]], axis=0)
            o2 = jnp.concatenate([o[wp:2 * wp], o[3 * wp:]], axis=0)
            oa = o1 - lam * o2
            o_ref[...] = (_rms(oa, subln_ref[...]) * (1.0 - lambda_init)).astype(BF16)
        else:
            o_ref[...] = o.astype(BF16)

    def even_tail(prefetch):
        diag_block(sa_sc)
        if prefetch:
            qk_next()
        finalize()

    def odd_tail(prefetch):
        qk(sb_sc, d0, wp, 0, half)
        qk(sb_sc, d0, tq, half, rows)
        softmax_pv(sa_sc, d0 - tq)
        if prefetch:
            qk_next()
        diag_block(sb_sc)
        finalize()

    last = nq - 1
    last_tail, other_tail = (even_tail, odd_tail) if last % 2 == 0 else (odd_tail, even_tail)
    pl.when(i == last)(functools.partial(last_tail, False))
    pl.when((i % 2 == last % 2) & (i != last))(functools.partial(last_tail, True))
    pl.when(i % 2 != last % 2)(functools.partial(other_tail, True))


def _flash(q, k, v, extras, *, blk, dual, lambda_init=0.0):
    bsz, heads, seq, dk = q.shape
    dv = v.shape[-1]
    dvp = dv + LANES
    assert blk % CHUNK == 0 and blk & (blk - 1) == 0 and seq % blk == 0
    rows = 2 * blk if dual else blk
    nq = seq // blk
    qspec = pl.BlockSpec((None, None, blk, dk), lambda b, h, i: (b, h, i, 0))
    qnspec = pl.BlockSpec((None, None, blk, dk),
                          lambda b, h, i: (b, h, jnp.minimum(i + 1, nq - 1), 0))
    kspec = pl.BlockSpec((None, None, seq, dk), lambda b, h, i: (b, h, 0, 0))
    vspec = pl.BlockSpec((None, None, seq, dv), lambda b, h, i: (b, h, 0, 0))
    ospec = pl.BlockSpec((None, blk, dv), lambda b, h, i: (b, i, h))
    wp = blk // 2
    frame_chunk = jnp.arange(wp, dtype=jnp.int32) // CHUNK
    tri = (frame_chunk[None, :] <= frame_chunk[:, None]).astype(F32)
    extras = [tri] + list(extras)
    espec = [pl.BlockSpec(e.shape, lambda b, h, i: (0, 0)) for e in extras]
    scratch = [pltpu.VMEM((rows, blk), F32), pltpu.VMEM((rows, blk), F32),
               pltpu.VMEM((rows, LANES), F32), pltpu.VMEM((rows, dvp), F32)]
    if dual:
        scratch = [pltpu.VMEM((rows, dk), BF16)] + scratch
    kern = functools.partial(_flash_kernel, tq=blk, nq=nq, dual=dual, lambda_init=lambda_init)
    return pl.pallas_call(
        kern,
        grid=(bsz, heads, nq),
        in_specs=[qspec, qnspec, kspec, vspec] + espec,
        out_specs=ospec,
        scratch_shapes=scratch,
        out_shape=jax.ShapeDtypeStruct((bsz, seq, heads * dv), BF16),
        compiler_params=pltpu.CompilerParams(
            dimension_semantics=("parallel", "parallel", "arbitrary"),
            vmem_limit_bytes=VMEM_LIMIT),
        name="flash_da" if dual else "flash_mla",
    )(q, q, k, v, *extras)


def _back_kernel(oa_ref, ob_ref, g_ref, x_ref, wa_ref, wb_ref, wo_ref, lnm_ref,
                 lnp_ref, wg_ref, wu_ref, wd_ref, lno_ref, o_ref, *, d, ff_chunk):
    d_ff = wg_ref.shape[1]
    tm = x_ref.shape[0]
    groups = [slice(r, r + tm // 2) for r in range(0, tm, tm // 2)]

    def mix(rs):
        y_a = jnp.dot(oa_ref[rs, :], wa_ref[...], preferred_element_type=F32)
        y_b = jnp.dot(ob_ref[rs, :], wb_ref[...], preferred_element_type=F32)
        merged = g_ref[rs, 0:d].astype(F32) * y_a + g_ref[rs, d:2 * d].astype(F32) * y_b
        return jnp.dot(merged.astype(BF16), wo_ref[...], preferred_element_type=F32)

    def ffn_chunk(h, c):
        w = min(ff_chunk, d_ff - c)
        gate = jnp.dot(h, wg_ref[:, c:c + w], preferred_element_type=F32)
        up = jnp.dot(h, wu_ref[:, c:c + w], preferred_element_type=F32)
        a = (jax.nn.silu(gate) * up).astype(BF16)
        return jnp.dot(a, wd_ref[c:c + w, :], preferred_element_type=F32)

    m = [mix(rs) for rs in groups]
    x1 = [x_ref[rs, :] + _rms(mg, lnm_ref[...]) for rs, mg in zip(groups, m)]
    h = [_rms(xg, lnp_ref[...]).astype(BF16) for xg in x1]
    f = [None] * len(groups)
    for c in range(0, d_ff, ff_chunk):
        for gi in range(len(groups)):
            part = ffn_chunk(h[gi], c)
            f[gi] = part if f[gi] is None else f[gi] + part
    for rs, xg, fg in zip(groups, x1, f):
        o_ref[rs, :] = xg + _rms(fg, lno_ref[...])


def _back(oa, ob, g, x2, wa, wb, wo, lnm, lnp, wg, wu, wd, lno, *, tm):
    t, d = x2.shape
    ff_chunk = 1024
    row = lambda w: pl.BlockSpec((tm, w), lambda i: (i, 0))
    weights = [wa, wb, wo, lnm, lnp, wg, wu, wd, lno]
    return pl.pallas_call(
        functools.partial(_back_kernel, d=d, ff_chunk=ff_chunk),
        grid=(t // tm,),
        in_specs=[row(d), row(d), row(2 * d), row(d)] + [_resident(w.shape) for w in weights],
        out_specs=row(d),
        out_shape=jax.ShapeDtypeStruct((t, d), F32),
        compiler_params=pltpu.CompilerParams(
            dimension_semantics=("parallel",), vmem_limit_bytes=VMEM_LIMIT),
        name="back",
    )(oa, ob, g, x2, *weights)


def _split_w_in(w, d):
    lat = MLA_Q_RANK + MLA_KV_RANK + ROPE_DIM
    zm_w = -(-lat // LANES) * LANES
    latent = jnp.pad(w[:, 3 * d:3 * d + lat].astype(BF16), ((0, 0), (0, zm_w - lat)))
    return [w[:, :3 * d].astype(BF16), latent, w[:, 3 * d + lat:].astype(BF16)]


def _pack_w_uq(w):
    r = w.shape[0]
    w = w.reshape(r, MLA_HEADS, MLA_QK_DIM)
    w = jnp.pad(w, ((0, 0), (0, 0), (0, MLA_HEAD_PAD - MLA_QK_DIM)))
    return w.reshape(r, MLA_HEADS * MLA_HEAD_PAD).astype(BF16)


def kernel(x, positions, ln_mix_pre, w_in, lambda_q1, lambda_k1, lambda_q2, lambda_k2, da_subln,
           q_a_norm, w_uq, kv_a_norm, w_ukv, w_proj_a, w_proj_b, w_o, ln_mix_post, ln_ffn_pre,
           w_ffn_gate, w_ffn_up, w_ffn_down, ln_ffn_post):
    bsz, seq, d = x.shape
    t = bsz * seq
    depth = w_in.shape[0]
    tm = 512

    half = ROPE_DIM // 2
    inv_freq = 1.0 / (ROPE_THETA ** (jnp.arange(half, dtype=F32) * (2.0 / ROPE_DIM)))
    inv_freq_lanes = jnp.tile(inv_freq, LANES // half).reshape(1, LANES)
    pos_lanes = jnp.broadcast_to(positions.reshape(t, 1), (t, LANES))

    x2 = x.reshape(t, d)
    for l in range(depth):
        lambda_init = 0.8 - 0.6 * math.exp(-0.3 * l)
        q_da, k_da, v_da, gates, q_b, k_b, v_b = _front(
            x2, pos_lanes, inv_freq_lanes, ln_mix_pre[l].reshape(1, d), _split_w_in(w_in[l], d),
            q_a_norm[l].reshape(1, -1), kv_a_norm[l].reshape(1, -1),
            _pack_w_uq(w_uq[l]), w_ukv[l].astype(BF16),
            bsz=bsz, seq=seq, d=d, tm=tm,
            qa_scale=DA_HEAD_DIM ** -0.5 * LOG2E, qb_scale=MLA_QK_DIM ** -0.5 * LOG2E)

        lam_p = jnp.stack([lambda_q1[l], lambda_k1[l], lambda_q2[l], lambda_k2[l]]).astype(F32)
        lam_p = jnp.pad(lam_p, ((0, 0), (0, LANES - DA_HEAD_DIM)))
        oa = _flash(q_da, k_da, v_da, [lam_p, da_subln[l].reshape(1, -1)], blk=1024, dual=True,
                    lambda_init=lambda_init)
        ob = _flash(q_b, k_b, v_b, [], blk=1024, dual=False)

        x2 = _back(oa.reshape(t, d), ob.reshape(t, d), gates, x2,
                   w_proj_a[l].astype(BF16), w_proj_b[l].astype(BF16), w_o[l].astype(BF16),
                   ln_mix_post[l].reshape(1, d), ln_ffn_pre[l].reshape(1, d),
                   w_ffn_gate[l].astype(BF16), w_ffn_up[l].astype(BF16),
                   w_ffn_down[l].astype(BF16), ln_ffn_post[l].reshape(1, d), tm=tm)
    return x2.reshape(bsz, seq, d)
```

```python
import functools
import math

import jax
import jax.numpy as jnp
from jax import lax
from jax.experimental import pallas as pl
from jax.experimental.pallas import tpu as pltpu

F32 = jnp.float32
BF16 = jnp.bfloat16

LANES = 128
CHUNK = 64
RMS_EPS = 1e-6
ROPE_THETA = 10000.0
ROPE_DIM = 64
MASK_VALUE = -1e30
LOG2E = math.log2(math.e)

DA_HEAD_DIM = 64
MLA_HEADS = 8
MLA_Q_RANK = 384
MLA_KV_RANK = 256
MLA_NOPE_DIM = 128
MLA_V_DIM = 128
MLA_QK_DIM = MLA_NOPE_DIM + ROPE_DIM
MLA_HEAD_PAD = 256

VMEM_LIMIT = 56 * 1024 * 1024


def _rms(x, w, eps=RMS_EPS):
    return x * lax.rsqrt(jnp.mean(x * x, axis=-1, keepdims=True) + eps) * w


def _rope_tile(x, cos, sin_signed, first_half):
    rot = jnp.where(first_half, pltpu.roll(x, 96, 1), pltpu.roll(x, 32, 1))
    return x * cos + rot * sin_signed


def _first_half_mask(shape):
    lane = lax.broadcasted_iota(jnp.int32, shape, 1)
    return (lane & (ROPE_DIM // 2)) == 0


def _ones_column(shape):
    lane = lax.broadcasted_iota(jnp.int32, shape, 1)
    return jnp.where(lane == 0, 1.0, 0.0).astype(BF16)


def _resident(shape):
    return pl.BlockSpec(shape, lambda *_: (0,) * len(shape), pipeline_mode=pl.Buffered(1))


def _front_kernel(x_ref, pos_ref, invf_ref, ln_ref, wqkv_ref, wlat_ref, wgate_ref,
                  qn_ref, kvn_ref, wq_ref, wkv_ref,
                  qa_ref, ka_ref, va_ref, g_ref, qb_ref, kb_ref, vb_ref,
                  *, d, nc, qa_scale, qb_scale):
    h = _rms(x_ref[...], ln_ref[...]).astype(BF16)
    ang = pos_ref[...].astype(F32) * invf_ref[...]
    first = _first_half_mask(ang.shape)
    cos = jnp.cos(ang)
    sin_plain = jnp.sin(ang)
    sin = jnp.where(first, -sin_plain, sin_plain)

    def mm(c0, width, w_ref=wqkv_ref):
        return jnp.dot(h, w_ref[:, c0:c0 + width], preferred_element_type=F32)

    for c in range(0, d, nc):
        z = mm(c, nc)
        for g in range(0, nc, LANES):
            r = _rope_tile(z[:, g:g + LANES], cos, sin, first) * qa_scale
            qa_ref[(c + g) // LANES] = r.astype(BF16)
    for c in range(0, d, nc):
        z = mm(d + c, nc)
        for g in range(0, nc, LANES):
            r = _rope_tile(z[:, g:g + LANES], cos, sin, first)
            ka_ref[(c + g) // LANES] = r.astype(BF16)
    for c in range(0, d, nc):
        z = mm(2 * d + c, nc)
        for g in range(0, nc, LANES):
            va_ref[(c + g) // LANES] = z[:, g:g + LANES].astype(BF16)
    for c in range(0, 2 * d, nc):
        g_ref[:, c:c + nc] = jax.nn.sigmoid(mm(c, nc, wgate_ref)).astype(BF16)

    zm = mm(0, wlat_ref.shape[1], wlat_ref)
    cq = _rms(zm[:, 0:MLA_Q_RANK], qn_ref[...]).astype(BF16)
    ckv = _rms(zm[:, MLA_Q_RANK:MLA_Q_RANK + MLA_KV_RANK], kvn_ref[...]).astype(BF16)
    kr0 = MLA_Q_RANK + MLA_KV_RANK
    k_rope = _rope_tile(zm[:, kr0:kr0 + LANES], cos, sin, first).astype(BF16)
    for hd in range(MLA_HEADS):
        c = hd * MLA_HEAD_PAD
        qh = jnp.dot(cq, wq_ref[:, c:c + MLA_HEAD_PAD], preferred_element_type=F32)
        qb_ref[hd, :, 0:LANES] = (qh[:, :LANES] * qb_scale).astype(BF16)
        qb_ref[hd, :, LANES:2 * LANES] = (
            _rope_tile(qh[:, LANES:], cos, sin, first) * qb_scale).astype(BF16)
        kvh = jnp.dot(ckv, wkv_ref[:, c:c + MLA_HEAD_PAD], preferred_element_type=F32)
        kb_ref[hd, :, 0:LANES] = kvh[:, :LANES].astype(BF16)
        kb_ref[hd, :, LANES:2 * LANES] = k_rope
        vb_ref[hd] = kvh[:, LANES:].astype(BF16)


def _head_major_spec(heads, tm, w, seq):
    per_seq = seq // tm
    return pl.BlockSpec((None, heads, tm, w), lambda i: (i // per_seq, 0, i % per_seq, 0))


def _front(x2, pos_lanes, inv_freq_lanes, ln, w_parts, qn, kvn, wq_p, wkv,
           *, bsz, seq, d, tm, qa_scale, qb_scale):
    t = x2.shape[0]
    row = lambda w: pl.BlockSpec((tm, w), lambda i: (i, 0))
    da_heads = d // LANES
    hm = lambda heads, w: _head_major_spec(heads, tm, w, seq)
    hm_shape = lambda heads, w: jax.ShapeDtypeStruct((bsz, heads, seq, w), BF16)
    kern = functools.partial(_front_kernel, d=d, nc=512, qa_scale=qa_scale, qb_scale=qb_scale)
    return pl.pallas_call(
        kern,
        grid=(t // tm,),
        in_specs=[row(d), row(LANES), _resident((1, LANES)), _resident((1, d))]
        + [_resident(w.shape) for w in w_parts]
        + [_resident(qn.shape), _resident(kvn.shape), _resident(wq_p.shape),
           _resident(wkv.shape)],
        out_specs=[hm(da_heads, LANES), hm(da_heads, LANES), hm(da_heads, LANES), row(2 * d),
                   hm(MLA_HEADS, MLA_HEAD_PAD), hm(MLA_HEADS, MLA_HEAD_PAD),
                   hm(MLA_HEADS, MLA_V_DIM)],
        out_shape=[hm_shape(da_heads, LANES)] * 3 + [jax.ShapeDtypeStruct((t, 2 * d), BF16)]
        + [hm_shape(MLA_HEADS, MLA_HEAD_PAD)] * 2 + [hm_shape(MLA_HEADS, MLA_V_DIM)],
        compiler_params=pltpu.CompilerParams(
            dimension_semantics=("parallel",), vmem_limit_bytes=VMEM_LIMIT),
        name="front",
    )(x2, pos_lanes, inv_freq_lanes, ln, *w_parts, qn, kvn, wq_p, wkv)


def _flash_kernel(q_ref, qn_ref, k_ref, v_ref, tri_ref, *rest, tq, nq, dual, lambda_init):
    if dual:
        lam_ref, subln_ref, o_ref, qz_sc, sa_sc, sb_sc, m_sc, acc_sc = rest
    else:
        o_ref, sa_sc, sb_sc, m_sc, acc_sc = rest
    i = pl.program_id(2)
    dv = o_ref.shape[-1]
    rows = m_sc.shape[0]
    half = rows // 2
    wp = tq // 2
    nt = (((1,), (1,)), ((), ()))

    def stack(q_in):
        lo = lax.broadcasted_iota(jnp.int32, q_in.shape, 1) < DA_HEAD_DIM
        zero = jnp.zeros_like(q_in)
        q1 = jnp.where(lo, q_in, zero)
        q2 = jnp.where(lo, zero, q_in)
        return jnp.concatenate([q1[0:wp], q2[0:wp], q1[wp:], q2[wp:]], axis=0)

    if dual:
        qz_sc[...] = stack(q_ref[...])
    m_sc[...] = jnp.full(m_sc.shape, -jnp.inf, F32)
    acc_sc[...] = jnp.zeros(acc_sc.shape, F32)
    q_src = qz_sc if dual else q_ref

    def qk(s_ref, key0, width=tq, r0=0, r1=rows):
        kk = k_ref[pl.ds(pl.multiple_of(key0, wp), width), :]
        s_ref[r0:r1, 0:width] = lax.dot_general(
            q_src[r0:r1, :], kk, nt, preferred_element_type=F32)

    def qk_next():
        qn = stack(qn_ref[...]) if dual else qn_ref[...]
        sa_sc[...] = lax.dot_general(qn, k_ref[0:tq, :], nt, preferred_element_type=F32)

    def chunk_causal(s):
        allowed = tri_ref[...] != 0.0
        groups = [jnp.where(allowed, s[g:g + wp], MASK_VALUE) for g in range(0, s.shape[0], wp)]
        return jnp.concatenate(groups, axis=0)

    def softmax_pv(s_ref, key0, width=tq, r0=0, r1=rows, diag=False):
        s = s_ref[r0:r1, 0:width]
        if diag:
            tail = chunk_causal(s[:, width - wp:])
            s = tail if width == wp else jnp.concatenate([s[:, 0:width - wp], tail], axis=1)
        m_prev = m_sc[r0:r1, :]
        m_new = jnp.maximum(m_prev, jnp.max(s, axis=1, keepdims=True))
        alpha = jnp.exp2(m_prev - m_new)
        p = jnp.exp2((s - jnp.concatenate([m_new] * (width // LANES), axis=1)).astype(BF16))
        vv = v_ref[pl.ds(pl.multiple_of(key0, wp), width), :]
        vv = jnp.concatenate([vv, _ones_column((width, LANES))], axis=1)
        pv = jnp.dot(p, vv, preferred_element_type=F32)
        alpha_w = jnp.concatenate([alpha] * (acc_sc.shape[1] // LANES), axis=1)
        acc_sc[r0:r1, :] = alpha_w * acc_sc[r0:r1, :] + pv
        m_sc[r0:r1, :] = m_new

    def diag_block(s_ref):
        softmax_pv(s_ref, d0, tq, half, rows, diag=True)
        softmax_pv(s_ref, d0, wp, 0, half, diag=True)

    @pl.when(i == 0)
    def _first_block_of_head():
        qk(sa_sc, 0)

    def pair(t, carry):
        qk(sb_sc, (2 * t + 1) * tq)
        softmax_pv(sa_sc, 2 * t * tq)
        qk(sa_sc, (2 * t + 2) * tq)
        softmax_pv(sb_sc, (2 * t + 1) * tq)
        return carry

    lax.fori_loop(0, i // 2, pair, 0)

    d0 = i * tq

    def finalize():
        o = acc_sc[:, 0:dv] / acc_sc[:, dv:dv + 1]
        if dual:
            lp = lam_ref[...]
            lam = (jnp.exp(jnp.sum(lp[0:1] * lp[1:2], axis=1, keepdims=True))
                   - jnp.exp(jnp.sum(lp[2:3] * lp[3:4], axis=1, keepdims=True))
                   + lambda_init)
            o1 = jnp.concatenate([o[0:wp], o[2 * wp:3 * wp]], axis=0)
            o2 = jnp.concatenate([o[wp:2 * wp], o[3 * wp:]], axis=0)
            oa = o1 - lam * o2
            o_ref[...] = (_rms(oa, subln_ref[...]) * (1.0 - lambda_init)).astype(BF16)
        else:
            o_ref[...] = o.astype(BF16)

    def even_tail(prefetch):
        diag_block(sa_sc)
        if prefetch:
            qk_next()
        finalize()

    def odd_tail(prefetch):
        qk(sb_sc, d0, wp, 0, half)
        qk(sb_sc, d0, tq, half, rows)
        softmax_pv(sa_sc, d0 - tq)
        diag_block(sb_sc)
        if prefetch:
            qk_next()
        finalize()

    last = nq - 1
    last_tail, other_tail = (even_tail, odd_tail) if last % 2 == 0 else (odd_tail, even_tail)
    pl.when(i == last)(functools.partial(last_tail, False))
    pl.when((i % 2 == last % 2) & (i != last))(functools.partial(last_tail, True))
    pl.when(i % 2 != last % 2)(functools.partial(other_tail, True))


def _flash(q, k, v, extras, *, blk, dual, lambda_init=0.0):
    bsz, heads, seq, dk = q.shape
    dv = v.shape[-1]
    dvp = dv + LANES
    assert blk % CHUNK == 0 and blk & (blk - 1) == 0 and seq % blk == 0
    rows = 2 * blk if dual else blk
    nq = seq // blk
    qspec = pl.BlockSpec((None, None, blk, dk), lambda b, h, i: (b, h, i, 0))
    qnspec = pl.BlockSpec((None, None, blk, dk),
                          lambda b, h, i: (b, h, jnp.minimum(i + 1, nq - 1), 0))
    kspec = pl.BlockSpec((None, None, seq, dk), lambda b, h, i: (b, h, 0, 0))
    vspec = pl.BlockSpec((None, None, seq, dv), lambda b, h, i: (b, h, 0, 0))
    ospec = pl.BlockSpec((None, blk, dv), lambda b, h, i: (b, i, h))
    wp = blk // 2
    frame_chunk = jnp.arange(wp, dtype=jnp.int32) // CHUNK
    tri = (frame_chunk[None, :] <= frame_chunk[:, None]).astype(F32)
    extras = [tri] + list(extras)
    espec = [pl.BlockSpec(e.shape, lambda b, h, i: (0, 0)) for e in extras]
    scratch = [pltpu.VMEM((rows, blk), F32), pltpu.VMEM((rows, blk), F32),
               pltpu.VMEM((rows, LANES), F32), pltpu.VMEM((rows, dvp), F32)]
    if dual:
        scratch = [pltpu.VMEM((rows, dk), BF16)] + scratch
    kern = functools.partial(_flash_kernel, tq=blk, nq=nq, dual=dual, lambda_init=lambda_init)
    return pl.pallas_call(
        kern,
        grid=(bsz, heads, nq),
        in_specs=[qspec, qnspec, kspec, vspec] + espec,
        out_specs=ospec,
        scratch_shapes=scratch,
        out_shape=jax.ShapeDtypeStruct((bsz, seq, heads * dv), BF16),
        compiler_params=pltpu.CompilerParams(
            dimension_semantics=("parallel", "parallel", "arbitrary"),
            vmem_limit_bytes=VMEM_LIMIT),
        name="flash_da" if dual else "flash_mla",
    )(q, q, k, v, *extras)


def _back_kernel(oa_ref, ob_ref, g_ref, x_ref, wa_ref, wb_ref, wo_ref, lnm_ref,
                 lnp_ref, wg_ref, wu_ref, wd_ref, lno_ref, o_ref, *, d, ff_chunk):
    d_ff = wg_ref.shape[1]
    tm = x_ref.shape[0]
    groups = [slice(r, r + tm // 2) for r in range(0, tm, tm // 2)]

    def mix(rs):
        y_a = jnp.dot(oa_ref[rs, :], wa_ref[...], preferred_element_type=F32)
        y_b = jnp.dot(ob_ref[rs, :], wb_ref[...], preferred_element_type=F32)
        merged = g_ref[rs, 0:d].astype(F32) * y_a + g_ref[rs, d:2 * d].astype(F32) * y_b
        return jnp.dot(merged.astype(BF16), wo_ref[...], preferred_element_type=F32)

    def ffn_chunk(h, c):
        w = min(ff_chunk, d_ff - c)
        gate = jnp.dot(h, wg_ref[:, c:c + w], preferred_element_type=F32)
        up = jnp.dot(h, wu_ref[:, c:c + w], preferred_element_type=F32)
        a = (jax.nn.silu(gate) * up).astype(BF16)
        return jnp.dot(a, wd_ref[c:c + w, :], preferred_element_type=F32)

    m = [mix(rs) for rs in groups]
    x1 = [x_ref[rs, :] + _rms(mg, lnm_ref[...]) for rs, mg in zip(groups, m)]
    h = [_rms(xg, lnp_ref[...]).astype(BF16) for xg in x1]
    f = [None] * len(groups)
    for c in range(0, d_ff, ff_chunk):
        for gi in range(len(groups)):
            part = ffn_chunk(h[gi], c)
            f[gi] = part if f[gi] is None else f[gi] + part
    for rs, xg, fg in zip(groups, x1, f):
        o_ref[rs, :] = xg + _rms(fg, lno_ref[...])


def _back(oa, ob, g, x2, wa, wb, wo, lnm, lnp, wg, wu, wd, lno, *, tm):
    t, d = x2.shape
    ff_chunk = 1024
    row = lambda w: pl.BlockSpec((tm, w), lambda i: (i, 0))
    weights = [wa, wb, wo, lnm, lnp, wg, wu, wd, lno]
    return pl.pallas_call(
        functools.partial(_back_kernel, d=d, ff_chunk=ff_chunk),
        grid=(t // tm,),
        in_specs=[row(d), row(d), row(2 * d), row(d)] + [_resident(w.shape) for w in weights],
        out_specs=row(d),
        out_shape=jax.ShapeDtypeStruct((t, d), F32),
        compiler_params=pltpu.CompilerParams(
            dimension_semantics=("parallel",), vmem_limit_bytes=VMEM_LIMIT),
        name="back",
    )(oa, ob, g, x2, *weights)


def _split_w_in(w, d):
    lat = MLA_Q_RANK + MLA_KV_RANK + ROPE_DIM
    zm_w = -(-lat // LANES) * LANES
    latent = jnp.pad(w[:, 3 * d:3 * d + lat].astype(BF16), ((0, 0), (0, zm_w - lat)))
    return [w[:, :3 * d].astype(BF16), latent, w[:, 3 * d + lat:].astype(BF16)]


def _pack_w_uq(w):
    r = w.shape[0]
    w = w.reshape(r, MLA_HEADS, MLA_QK_DIM)
    w = jnp.pad(w, ((0, 0), (0, 0), (0, MLA_HEAD_PAD - MLA_QK_DIM)))
    return w.reshape(r, MLA_HEADS * MLA_HEAD_PAD).astype(BF16)


def kernel(x, positions, ln_mix_pre, w_in, lambda_q1, lambda_k1, lambda_q2, lambda_k2, da_subln,
           q_a_norm, w_uq, kv_a_norm, w_ukv, w_proj_a, w_proj_b, w_o, ln_mix_post, ln_ffn_pre,
           w_ffn_gate, w_ffn_up, w_ffn_down, ln_ffn_post):
    bsz, seq, d = x.shape
    t = bsz * seq
    depth = w_in.shape[0]
    tm = 512

    half = ROPE_DIM // 2
    inv_freq = 1.0 / (ROPE_THETA ** (jnp.arange(half, dtype=F32) * (2.0 / ROPE_DIM)))
    inv_freq_lanes = jnp.tile(inv_freq, LANES // half).reshape(1, LANES)
    pos_lanes = jnp.broadcast_to(positions.reshape(t, 1), (t, LANES))

    x2 = x.reshape(t, d)
    for l in range(depth):
        lambda_init = 0.8 - 0.6 * math.exp(-0.3 * l)
        q_da, k_da, v_da, gates, q_b, k_b, v_b = _front(
            x2, pos_lanes, inv_freq_lanes, ln_mix_pre[l].reshape(1, d), _split_w_in(w_in[l], d),
            q_a_norm[l].reshape(1, -1), kv_a_norm[l].reshape(1, -1),
            _pack_w_uq(w_uq[l]), w_ukv[l].astype(BF16),
            bsz=bsz, seq=seq, d=d, tm=tm,
            qa_scale=DA_HEAD_DIM ** -0.5 * LOG2E, qb_scale=MLA_QK_DIM ** -0.5 * LOG2E)

        lam_p = jnp.stack([lambda_q1[l], lambda_k1[l], lambda_q2[l], lambda_k2[l]]).astype(F32)
        lam_p = jnp.pad(lam_p, ((0, 0), (0, LANES - DA_HEAD_DIM)))
        oa = _flash(q_da, k_da, v_da, [lam_p, da_subln[l].reshape(1, -1)], blk=1024, dual=True,
                    lambda_init=lambda_init)
        ob = _flash(q_b, k_b, v_b, [], blk=1024, dual=False)

        x2 = _back(oa.reshape(t, d), ob.reshape(t, d), gates, x2,
                   w_proj_a[l].astype(BF16), w_proj_b[l].astype(BF16), w_o[l].astype(BF16),
                   ln_mix_post[l].reshape(1, d), ln_ffn_pre[l].reshape(1, d),
                   w_ffn_gate[l].astype(BF16), w_ffn_up[l].astype(BF16),
                   w_ffn_down[l].astype(BF16), ln_ffn_post[l].reshape(1, d), tm=tm)
    return x2.reshape(bsz, seq, d)
```

```python
import functools
import math

import jax
import jax.numpy as jnp
from jax import lax
from jax.experimental import pallas as pl
from jax.experimental.pallas import tpu as pltpu

F32 = jnp.float32
BF16 = jnp.bfloat16

LANES = 128
CHUNK = 64
RMS_EPS = 1e-6
ROPE_THETA = 10000.0
ROPE_DIM = 64
MASK_VALUE = -1e30
LOG2E = math.log2(math.e)

DA_HEAD_DIM = 64
MLA_HEADS = 8
MLA_Q_RANK = 384
MLA_KV_RANK = 256
MLA_NOPE_DIM = 128
MLA_V_DIM = 128
MLA_QK_DIM = MLA_NOPE_DIM + ROPE_DIM
MLA_HEAD_PAD = 256

VMEM_LIMIT = 56 * 1024 * 1024


def _rms(x, w, eps=RMS_EPS):
    return x * lax.rsqrt(jnp.mean(x * x, axis=-1, keepdims=True) + eps) * w


def _rope_tile(x, cos, sin_signed, first_half):
    rot = jnp.where(first_half, pltpu.roll(x, 96, 1), pltpu.roll(x, 32, 1))
    return x * cos + rot * sin_signed


def _first_half_mask(shape):
    lane = lax.broadcasted_iota(jnp.int32, shape, 1)
    return (lane & (ROPE_DIM // 2)) == 0


def _ones_column(shape):
    lane = lax.broadcasted_iota(jnp.int32, shape, 1)
    return jnp.where(lane == 0, 1.0, 0.0).astype(BF16)


def _resident(shape):
    return pl.BlockSpec(shape, lambda *_: (0,) * len(shape), pipeline_mode=pl.Buffered(1))


def _front_kernel(x_ref, pos_ref, invf_ref, ln_ref, wqkv_ref, wlat_ref, wgate_ref,
                  qn_ref, kvn_ref, wq_ref, wkv_ref,
                  qa_ref, ka_ref, va_ref, g_ref, qb_ref, kb_ref, vb_ref,
                  *, d, nc, qa_scale, qb_scale):
    h = _rms(x_ref[...], ln_ref[...]).astype(BF16)
    ang = pos_ref[...].astype(F32) * invf_ref[...]
    first = _first_half_mask(ang.shape)
    cos = jnp.cos(ang)
    sin_plain = jnp.sin(ang)
    sin = jnp.where(first, -sin_plain, sin_plain)

    def mm(c0, width, w_ref=wqkv_ref):
        return jnp.dot(h, w_ref[:, c0:c0 + width], preferred_element_type=F32)

    for c in range(0, d, nc):
        z = mm(c, nc)
        for g in range(0, nc, LANES):
            r = _rope_tile(z[:, g:g + LANES], cos, sin, first) * qa_scale
            qa_ref[(c + g) // LANES] = r.astype(BF16)
    for c in range(0, d, nc):
        z = mm(d + c, nc)
        for g in range(0, nc, LANES):
            r = _rope_tile(z[:, g:g + LANES], cos, sin, first)
            ka_ref[(c + g) // LANES] = r.astype(BF16)
    for c in range(0, d, nc):
        z = mm(2 * d + c, nc)
        for g in range(0, nc, LANES):
            va_ref[(c + g) // LANES] = z[:, g:g + LANES].astype(BF16)
    for c in range(0, 2 * d, nc):
        g_ref[:, c:c + nc] = jax.nn.sigmoid(mm(c, nc, wgate_ref)).astype(BF16)

    zm = mm(0, wlat_ref.shape[1], wlat_ref)
    cq = _rms(zm[:, 0:MLA_Q_RANK], qn_ref[...]).astype(BF16)
    ckv = _rms(zm[:, MLA_Q_RANK:MLA_Q_RANK + MLA_KV_RANK], kvn_ref[...]).astype(BF16)
    kr0 = MLA_Q_RANK + MLA_KV_RANK
    k_rope = _rope_tile(zm[:, kr0:kr0 + LANES], cos, sin, first).astype(BF16)
    for hd in range(MLA_HEADS):
        c = hd * MLA_HEAD_PAD
        qh = jnp.dot(cq, wq_ref[:, c:c + MLA_HEAD_PAD], preferred_element_type=F32)
        qb_ref[hd, :, 0:LANES] = (qh[:, :LANES] * qb_scale).astype(BF16)
        qb_ref[hd, :, LANES:2 * LANES] = (
            _rope_tile(qh[:, LANES:], cos, sin, first) * qb_scale).astype(BF16)
        kvh = jnp.dot(ckv, wkv_ref[:, c:c + MLA_HEAD_PAD], preferred_element_type=F32)
        kb_ref[hd, :, 0:LANES] = kvh[:, :LANES].astype(BF16)
        kb_ref[hd, :, LANES:2 * LANES] = k_rope
        vb_ref[hd] = kvh[:, LANES:].astype(BF16)


def _head_major_spec(heads, tm, w, seq):
    per_seq = seq // tm
    return pl.BlockSpec((None, heads, tm, w), lambda i: (i // per_seq, 0, i % per_seq, 0))


def _front(x2, pos_lanes, inv_freq_lanes, ln, w_parts, qn, kvn, wq_p, wkv,
           *, bsz, seq, d, tm, qa_scale, qb_scale):
    t = x2.shape[0]
    row = lambda w: pl.BlockSpec((tm, w), lambda i: (i, 0))
    da_heads = d // LANES
    hm = lambda heads, w: _head_major_spec(heads, tm, w, seq)
    hm_shape = lambda heads, w: jax.ShapeDtypeStruct((bsz, heads, seq, w), BF16)
    kern = functools.partial(_front_kernel, d=d, nc=512, qa_scale=qa_scale, qb_scale=qb_scale)
    return pl.pallas_call(
        kern,
        grid=(t // tm,),
        in_specs=[row(d), row(LANES), _resident((1, LANES)), _resident((1, d))]
        + [_resident(w.shape) for w in w_parts]
        + [_resident(qn.shape), _resident(kvn.shape), _resident(wq_p.shape),
           _resident(wkv.shape)],
        out_specs=[hm(da_heads, LANES), hm(da_heads, LANES), hm(da_heads, LANES), row(2 * d),
                   hm(MLA_HEADS, MLA_HEAD_PAD), hm(MLA_HEADS, MLA_HEAD_PAD),
                   hm(MLA_HEADS, MLA_V_DIM)],
        out_shape=[hm_shape(da_heads, LANES)] * 3 + [jax.ShapeDtypeStruct((t, 2 * d), BF16)]
        + [hm_shape(MLA_HEADS, MLA_HEAD_PAD)] * 2 + [hm_shape(MLA_HEADS, MLA_V_DIM)],
        compiler_params=pltpu.CompilerParams(
            dimension_semantics=("parallel",), vmem_limit_bytes=VMEM_LIMIT),
        name="front",
    )(x2, pos_lanes, inv_freq_lanes, ln, *w_parts, qn, kvn, wq_p, wkv)


def _flash_kernel(q_ref, qn_ref, k_ref, v_ref, tri_ref, *rest, tq, nq, dual, lambda_init):
    if dual:
        lam_ref, subln_ref, o_ref, qz_sc, sa_sc, sb_sc, m_sc, acc_sc = rest
    else:
        o_ref, sa_sc, sb_sc, m_sc, acc_sc = rest
    i = pl.program_id(2)
    dv = o_ref.shape[-1]
    rows = m_sc.shape[0]
    half = rows // 2
    wp = tq // 2
    nt = (((1,), (1,)), ((), ()))

    def stack(q_in):
        lo = lax.broadcasted_iota(jnp.int32, q_in.shape, 1) < DA_HEAD_DIM
        zero = jnp.zeros_like(q_in)
        q1 = jnp.where(lo, q_in, zero)
        q2 = jnp.where(lo, zero, q_in)
        return jnp.concatenate([q1[0:wp], q2[0:wp], q1[wp:], q2[wp:]], axis=0)

    if dual:
        qz_sc[...] = stack(q_ref[...])
    m_sc[...] = jnp.full(m_sc.shape, -jnp.inf, F32)
    acc_sc[...] = jnp.zeros(acc_sc.shape, F32)
    q_src = qz_sc if dual else q_ref

    def qk(s_ref, key0, width=tq, r0=0, r1=rows):
        kk = k_ref[pl.ds(pl.multiple_of(key0, wp), width), :]
        s_ref[r0:r1, 0:width] = lax.dot_general(
            q_src[r0:r1, :], kk, nt, preferred_element_type=F32)

    def qk_next():
        qn = stack(qn_ref[...]) if dual else qn_ref[...]
        sa_sc[...] = lax.dot_general(qn, k_ref[0:tq, :], nt, preferred_element_type=F32)

    def chunk_causal(s):
        allowed = tri_ref[...] != 0.0
        groups = [jnp.where(allowed, s[g:g + wp], MASK_VALUE) for g in range(0, s.shape[0], wp)]
        return jnp.concatenate(groups, axis=0)

    def softmax_pv(s_ref, key0, width=tq, r0=0, r1=rows, diag=False):
        s = s_ref[r0:r1, 0:width]
        if diag:
            tail = chunk_causal(s[:, width - wp:])
            s = tail if width == wp else jnp.concatenate([s[:, 0:width - wp], tail], axis=1)
        m_prev = m_sc[r0:r1, :]
        m_new = jnp.maximum(m_prev, jnp.max(s, axis=1, keepdims=True))
        alpha = jnp.exp2(m_prev - m_new)
        p = jnp.exp2((s - jnp.concatenate([m_new] * (width // LANES), axis=1)).astype(BF16))
        vv = v_ref[pl.ds(pl.multiple_of(key0, wp), width), :]
        vv = jnp.concatenate([vv, _ones_column((width, LANES))], axis=1)
        pv = jnp.dot(p, vv, preferred_element_type=F32)
        alpha_w = jnp.concatenate([alpha] * (acc_sc.shape[1] // LANES), axis=1)
        acc_sc[r0:r1, :] = alpha_w * acc_sc[r0:r1, :] + pv
        m_sc[r0:r1, :] = m_new

    @pl.when(i == 0)
    def _first_block_of_head():
        qk(sa_sc, 0)

    def pair(t, carry):
        qk(sb_sc, (2 * t + 1) * tq)
        softmax_pv(sa_sc, 2 * t * tq)
        qk(sa_sc, (2 * t + 2) * tq)
        softmax_pv(sb_sc, (2 * t + 1) * tq)
        return carry

    lax.fori_loop(0, i // 2, pair, 0)

    d0 = i * tq

    def finalize(part):
        r0 = part * half
        o = acc_sc[r0:r0 + half, 0:dv] / acc_sc[r0:r0 + half, dv:dv + 1]
        if dual:
            lp = lam_ref[...]
            lam = (jnp.exp(jnp.sum(lp[0:1] * lp[1:2], axis=1, keepdims=True))
                   - jnp.exp(jnp.sum(lp[2:3] * lp[3:4], axis=1, keepdims=True))
                   + lambda_init)
            oa = o[0:wp] - lam * o[wp:]
            o = _rms(oa, subln_ref[...]) * (1.0 - lambda_init)
        o_ref[part * wp:(part + 1) * wp, :] = o.astype(BF16)

    def diag_and_finish(s_ref, prefetch):
        softmax_pv(s_ref, d0, tq, half, rows, diag=True)
        finalize(1)
        softmax_pv(s_ref, d0, wp, 0, half, diag=True)
        if prefetch:
            qk_next()
        finalize(0)

    def even_tail(prefetch):
        diag_and_finish(sa_sc, prefetch)

    def odd_tail(prefetch):
        qk(sb_sc, d0, wp, 0, half)
        qk(sb_sc, d0, tq, half, rows)
        softmax_pv(sa_sc, d0 - tq)
        diag_and_finish(sb_sc, prefetch)

    last = nq - 1
    last_tail, other_tail = (even_tail, odd_tail) if last % 2 == 0 else (odd_tail, even_tail)
    pl.when(i == last)(functools.partial(last_tail, False))
    pl.when((i % 2 == last % 2) & (i != last))(functools.partial(last_tail, True))
    pl.when(i % 2 != last % 2)(functools.partial(other_tail, True))


def _flash(q, k, v, extras, *, blk, dual, lambda_init=0.0):
    bsz, heads, seq, dk = q.shape
    dv = v.shape[-1]
    dvp = dv + LANES
    assert blk % CHUNK == 0 and blk & (blk - 1) == 0 and seq % blk == 0
    rows = 2 * blk if dual else blk
    nq = seq // blk
    qspec = pl.BlockSpec((None, None, blk, dk), lambda b, h, i: (b, h, i, 0))
    qnspec = pl.BlockSpec((None, None, blk, dk),
                          lambda b, h, i: (b, h, jnp.minimum(i + 1, nq - 1), 0))
    kspec = pl.BlockSpec((None, None, seq, dk), lambda b, h, i: (b, h, 0, 0))
    vspec = pl.BlockSpec((None, None, seq, dv), lambda b, h, i: (b, h, 0, 0))
    ospec = pl.BlockSpec((None, blk, dv), lambda b, h, i: (b, i, h))
    wp = blk // 2
    frame_chunk = jnp.arange(wp, dtype=jnp.int32) // CHUNK
    tri = (frame_chunk[None, :] <= frame_chunk[:, None]).astype(F32)
    extras = [tri] + list(extras)
    espec = [pl.BlockSpec(e.shape, lambda b, h, i: (0, 0)) for e in extras]
    scratch = [pltpu.VMEM((rows, blk), F32), pltpu.VMEM((rows, blk), F32),
               pltpu.VMEM((rows, LANES), F32), pltpu.VMEM((rows, dvp), F32)]
    if dual:
        scratch = [pltpu.VMEM((rows, dk), BF16)] + scratch
    kern = functools.partial(_flash_kernel, tq=blk, nq=nq, dual=dual, lambda_init=lambda_init)
    return pl.pallas_call(
        kern,
        grid=(bsz, heads, nq),
        in_specs=[qspec, qnspec, kspec, vspec] + espec,
        out_specs=ospec,
        scratch_shapes=scratch,
        out_shape=jax.ShapeDtypeStruct((bsz, seq, heads * dv), BF16),
        compiler_params=pltpu.CompilerParams(
            dimension_semantics=("parallel", "parallel", "arbitrary"),
            vmem_limit_bytes=VMEM_LIMIT),
        name="flash_da" if dual else "flash_mla",
    )(q, q, k, v, *extras)


def _back_kernel(oa_ref, ob_ref, g_ref, x_ref, wa_ref, wb_ref, wo_ref, lnm_ref,
                 lnp_ref, wg_ref, wu_ref, wd_ref, lno_ref, o_ref, *, d, ff_chunk):
    d_ff = wg_ref.shape[1]
    tm = x_ref.shape[0]
    groups = [slice(r, r + tm // 2) for r in range(0, tm, tm // 2)]

    def mix(rs):
        y_a = jnp.dot(oa_ref[rs, :], wa_ref[...], preferred_element_type=F32)
        y_b = jnp.dot(ob_ref[rs, :], wb_ref[...], preferred_element_type=F32)
        merged = g_ref[rs, 0:d].astype(F32) * y_a + g_ref[rs, d:2 * d].astype(F32) * y_b
        return jnp.dot(merged.astype(BF16), wo_ref[...], preferred_element_type=F32)

    def ffn_chunk(h, c):
        w = min(ff_chunk, d_ff - c)
        gate = jnp.dot(h, wg_ref[:, c:c + w], preferred_element_type=F32)
        up = jnp.dot(h, wu_ref[:, c:c + w], preferred_element_type=F32)
        a = (jax.nn.silu(gate) * up).astype(BF16)
        return jnp.dot(a, wd_ref[c:c + w, :], preferred_element_type=F32)

    m = [mix(rs) for rs in groups]
    x1 = [x_ref[rs, :] + _rms(mg, lnm_ref[...]) for rs, mg in zip(groups, m)]
    h = [_rms(xg, lnp_ref[...]).astype(BF16) for xg in x1]
    f = [None] * len(groups)
    for c in range(0, d_ff, ff_chunk):
        for gi in range(len(groups)):
            part = ffn_chunk(h[gi], c)
            f[gi] = part if f[gi] is None else f[gi] + part
    for rs, xg, fg in zip(groups, x1, f):
        o_ref[rs, :] = xg + _rms(fg, lno_ref[...])


def _back(oa, ob, g, x2, wa, wb, wo, lnm, lnp, wg, wu, wd, lno, *, tm):
    t, d = x2.shape
    ff_chunk = 1024
    row = lambda w: pl.BlockSpec((tm, w), lambda i: (i, 0))
    weights = [wa, wb, wo, lnm, lnp, wg, wu, wd, lno]
    return pl.pallas_call(
        functools.partial(_back_kernel, d=d, ff_chunk=ff_chunk),
        grid=(t // tm,),
        in_specs=[row(d), row(d), row(2 * d), row(d)] + [_resident(w.shape) for w in weights],
        out_specs=row(d),
        out_shape=jax.ShapeDtypeStruct((t, d), F32),
        compiler_params=pltpu.CompilerParams(
            dimension_semantics=("parallel",), vmem_limit_bytes=VMEM_LIMIT),
        name="back",
    )(oa, ob, g, x2, *weights)


def _split_w_in(w, d):
    lat = MLA_Q_RANK + MLA_KV_RANK + ROPE_DIM
    zm_w = -(-lat // LANES) * LANES
    latent = jnp.pad(w[:, 3 * d:3 * d + lat].astype(BF16), ((0, 0), (0, zm_w - lat)))
    return [w[:, :3 * d].astype(BF16), latent, w[:, 3 * d + lat:].astype(BF16)]


def _pack_w_uq(w):
    r = w.shape[0]
    w = w.reshape(r, MLA_HEADS, MLA_QK_DIM)
    w = jnp.pad(w, ((0, 0), (0, 0), (0, MLA_HEAD_PAD - MLA_QK_DIM)))
    return w.reshape(r, MLA_HEADS * MLA_HEAD_PAD).astype(BF16)


def kernel(x, positions, ln_mix_pre, w_in, lambda_q1, lambda_k1, lambda_q2, lambda_k2, da_subln,
           q_a_norm, w_uq, kv_a_norm, w_ukv, w_proj_a, w_proj_b, w_o, ln_mix_post, ln_ffn_pre,
           w_ffn_gate, w_ffn_up, w_ffn_down, ln_ffn_post):
    bsz, seq, d = x.shape
    t = bsz * seq
    depth = w_in.shape[0]
    tm = 512

    half = ROPE_DIM // 2
    inv_freq = 1.0 / (ROPE_THETA ** (jnp.arange(half, dtype=F32) * (2.0 / ROPE_DIM)))
    inv_freq_lanes = jnp.tile(inv_freq, LANES // half).reshape(1, LANES)
    pos_lanes = jnp.broadcast_to(positions.reshape(t, 1), (t, LANES))

    x2 = x.reshape(t, d)
    for l in range(depth):
        lambda_init = 0.8 - 0.6 * math.exp(-0.3 * l)
        q_da, k_da, v_da, gates, q_b, k_b, v_b = _front(
            x2, pos_lanes, inv_freq_lanes, ln_mix_pre[l].reshape(1, d), _split_w_in(w_in[l], d),
            q_a_norm[l].reshape(1, -1), kv_a_norm[l].reshape(1, -1),
            _pack_w_uq(w_uq[l]), w_ukv[l].astype(BF16),
            bsz=bsz, seq=seq, d=d, tm=tm,
            qa_scale=DA_HEAD_DIM ** -0.5 * LOG2E, qb_scale=MLA_QK_DIM ** -0.5 * LOG2E)

        lam_p = jnp.stack([lambda_q1[l], lambda_k1[l], lambda_q2[l], lambda_k2[l]]).astype(F32)
        lam_p = jnp.pad(lam_p, ((0, 0), (0, LANES - DA_HEAD_DIM)))
        oa = _flash(q_da, k_da, v_da, [lam_p, da_subln[l].reshape(1, -1)], blk=1024, dual=True,
                    lambda_init=lambda_init)
        ob = _flash(q_b, k_b, v_b, [], blk=1024, dual=False)

        x2 = _back(oa.reshape(t, d), ob.reshape(t, d), gates, x2,
                   w_proj_a[l].astype(BF16), w_proj_b[l].astype(BF16), w_o[l].astype(BF16),
                   ln_mix_post[l].reshape(1, d), ln_ffn_pre[l].reshape(1, d),
                   w_ffn_gate[l].astype(BF16), w_ffn_up[l].astype(BF16),
                   w_ffn_down[l].astype(BF16), ln_ffn_post[l].reshape(1, d), tm=tm)
    return x2.reshape(bsz, seq, d)
```

```python
import functools
import math

import jax
import jax.numpy as jnp
from jax import lax
from jax.experimental import pallas as pl
from jax.experimental.pallas import tpu as pltpu

F32 = jnp.float32
BF16 = jnp.bfloat16

LANES = 128
CHUNK = 64
RMS_EPS = 1e-6
ROPE_THETA = 10000.0
ROPE_DIM = 64
MASK_VALUE = -1e30
LOG2E = math.log2(math.e)

DA_HEAD_DIM = 64
MLA_HEADS = 8
MLA_Q_RANK = 384
MLA_KV_RANK = 256
MLA_NOPE_DIM = 128
MLA_V_DIM = 128
MLA_QK_DIM = MLA_NOPE_DIM + ROPE_DIM
MLA_HEAD_PAD = 256

VMEM_LIMIT = 56 * 1024 * 1024


def _rms(x, w, eps=RMS_EPS):
    return x * lax.rsqrt(jnp.mean(x * x, axis=-1, keepdims=True) + eps) * w


def _rope_tile(x, cos, sin_signed, first_half):
    rot = jnp.where(first_half, pltpu.roll(x, 96, 1), pltpu.roll(x, 32, 1))
    return x * cos + rot * sin_signed


def _first_half_mask(shape):
    lane = lax.broadcasted_iota(jnp.int32, shape, 1)
    return (lane & (ROPE_DIM // 2)) == 0


def _ones_column(shape):
    lane = lax.broadcasted_iota(jnp.int32, shape, 1)
    return jnp.where(lane == 0, 1.0, 0.0).astype(BF16)


def _resident(shape):
    return pl.BlockSpec(shape, lambda *_: (0,) * len(shape), pipeline_mode=pl.Buffered(1))


def _front_kernel(x_ref, pos_ref, invf_ref, ln_ref, wqkv_ref, wlat_ref, wgate_ref,
                  qn_ref, kvn_ref, wq_ref, wkv_ref,
                  qa_ref, ka_ref, va_ref, g_ref, qb_ref, kb_ref, vb_ref,
                  *, d, nc, qa_scale, qb_scale):
    h = _rms(x_ref[...], ln_ref[...]).astype(BF16)
    ang = pos_ref[...].astype(F32) * invf_ref[...]
    first = _first_half_mask(ang.shape)
    cos = jnp.cos(ang)
    sin_plain = jnp.sin(ang)
    sin = jnp.where(first, -sin_plain, sin_plain)

    def mm(c0, width, w_ref=wqkv_ref):
        return jnp.dot(h, w_ref[:, c0:c0 + width], preferred_element_type=F32)

    for c in range(0, d, nc):
        z = mm(c, nc)
        for g in range(0, nc, LANES):
            r = _rope_tile(z[:, g:g + LANES], cos, sin, first) * qa_scale
            qa_ref[(c + g) // LANES] = r.astype(BF16)
    for c in range(0, d, nc):
        z = mm(d + c, nc)
        for g in range(0, nc, LANES):
            r = _rope_tile(z[:, g:g + LANES], cos, sin, first)
            ka_ref[(c + g) // LANES] = r.astype(BF16)
    for c in range(0, d, nc):
        z = mm(2 * d + c, nc)
        for g in range(0, nc, LANES):
            va_ref[(c + g) // LANES] = z[:, g:g + LANES].astype(BF16)
    for c in range(0, 2 * d, nc):
        g_ref[:, c:c + nc] = jax.nn.sigmoid(mm(c, nc, wgate_ref)).astype(BF16)

    zm = mm(0, wlat_ref.shape[1], wlat_ref)
    cq = _rms(zm[:, 0:MLA_Q_RANK], qn_ref[...]).astype(BF16)
    ckv = _rms(zm[:, MLA_Q_RANK:MLA_Q_RANK + MLA_KV_RANK], kvn_ref[...]).astype(BF16)
    kr0 = MLA_Q_RANK + MLA_KV_RANK
    k_rope = _rope_tile(zm[:, kr0:kr0 + LANES], cos, sin, first).astype(BF16)
    for hd in range(MLA_HEADS):
        c = hd * MLA_HEAD_PAD
        qh = jnp.dot(cq, wq_ref[:, c:c + MLA_HEAD_PAD], preferred_element_type=F32)
        qb_ref[hd, :, 0:LANES] = (qh[:, :LANES] * qb_scale).astype(BF16)
        qb_ref[hd, :, LANES:2 * LANES] = (
            _rope_tile(qh[:, LANES:], cos, sin, first) * qb_scale).astype(BF16)
        kvh = jnp.dot(ckv, wkv_ref[:, c:c + MLA_HEAD_PAD], preferred_element_type=F32)
        kb_ref[hd, :, 0:LANES] = kvh[:, :LANES].astype(BF16)
        kb_ref[hd, :, LANES:2 * LANES] = k_rope
        vb_ref[hd] = kvh[:, LANES:].astype(BF16)


def _head_major_spec(heads, tm, w, seq):
    per_seq = seq // tm
    return pl.BlockSpec((None, heads, tm, w), lambda i: (i // per_seq, 0, i % per_seq, 0))


def _front(x2, pos_lanes, inv_freq_lanes, ln, w_parts, qn, kvn, wq_p, wkv,
           *, bsz, seq, d, tm, qa_scale, qb_scale):
    t = x2.shape[0]
    row = lambda w: pl.BlockSpec((tm, w), lambda i: (i, 0))
    da_heads = d // LANES
    hm = lambda heads, w: _head_major_spec(heads, tm, w, seq)
    hm_shape = lambda heads, w: jax.ShapeDtypeStruct((bsz, heads, seq, w), BF16)
    kern = functools.partial(_front_kernel, d=d, nc=512, qa_scale=qa_scale, qb_scale=qb_scale)
    return pl.pallas_call(
        kern,
        grid=(t // tm,),
        in_specs=[row(d), row(LANES), _resident((1, LANES)), _resident((1, d))]
        + [_resident(w.shape) for w in w_parts]
        + [_resident(qn.shape), _resident(kvn.shape), _resident(wq_p.shape),
           _resident(wkv.shape)],
        out_specs=[hm(da_heads, LANES), hm(da_heads, LANES), hm(da_heads, LANES), row(2 * d),
                   hm(MLA_HEADS, MLA_HEAD_PAD), hm(MLA_HEADS, MLA_HEAD_PAD),
                   hm(MLA_HEADS, MLA_V_DIM)],
        out_shape=[hm_shape(da_heads, LANES)] * 3 + [jax.ShapeDtypeStruct((t, 2 * d), BF16)]
        + [hm_shape(MLA_HEADS, MLA_HEAD_PAD)] * 2 + [hm_shape(MLA_HEADS, MLA_V_DIM)],
        compiler_params=pltpu.CompilerParams(
            dimension_semantics=("parallel",), vmem_limit_bytes=VMEM_LIMIT),
        name="front",
    )(x2, pos_lanes, inv_freq_lanes, ln, *w_parts, qn, kvn, wq_p, wkv)


def _flash_kernel(q_ref, qn_ref, k_ref, v_ref, tri_ref, *rest, tq, nq, dual, lambda_init):
    if dual:
        lam_ref, subln_ref, o_ref, qz_sc, sa_sc, sb_sc, m_sc, acc_sc = rest
    else:
        o_ref, sa_sc, sb_sc, m_sc, acc_sc = rest
    i = pl.program_id(2)
    dv = o_ref.shape[-1]
    rows = m_sc.shape[0]
    half = rows // 2
    wp = tq // 2
    nt = (((1,), (1,)), ((), ()))

    def stack(q_in):
        lo = lax.broadcasted_iota(jnp.int32, q_in.shape, 1) < DA_HEAD_DIM
        zero = jnp.zeros_like(q_in)
        q1 = jnp.where(lo, q_in, zero)
        q2 = jnp.where(lo, zero, q_in)
        return jnp.concatenate([q1[0:wp], q2[0:wp], q1[wp:], q2[wp:]], axis=0)

    if dual:
        qz_sc[...] = stack(q_ref[...])
    m_sc[...] = jnp.full(m_sc.shape, -jnp.inf, F32)
    acc_sc[...] = jnp.zeros(acc_sc.shape, F32)
    q_src = qz_sc if dual else q_ref

    def qk(s_ref, key0, width=tq, r0=0, r1=rows, c0=0):
        kk = k_ref[pl.ds(pl.multiple_of(key0, wp), width), :]
        s_ref[r0:r1, c0:c0 + width] = lax.dot_general(
            q_src[r0:r1, :], kk, nt, preferred_element_type=F32)

    def qk_next(c0=0, c1=tq):
        qn = stack(qn_ref[...]) if dual else qn_ref[...]
        sa_sc[:, c0:c1] = lax.dot_general(qn, k_ref[c0:c1, :], nt, preferred_element_type=F32)

    def chunk_causal(s):
        allowed = tri_ref[...] != 0.0
        groups = [jnp.where(allowed, s[g:g + wp], MASK_VALUE) for g in range(0, s.shape[0], wp)]
        return jnp.concatenate(groups, axis=0)

    def softmax_pv(s_ref, key0, width=tq, r0=0, r1=rows, diag=False):
        s = s_ref[r0:r1, 0:width]
        if diag:
            tail = chunk_causal(s[:, width - wp:])
            s = tail if width == wp else jnp.concatenate([s[:, 0:width - wp], tail], axis=1)
        m_prev = m_sc[r0:r1, :]
        m_new = jnp.maximum(m_prev, jnp.max(s, axis=1, keepdims=True))
        alpha = jnp.exp2(m_prev - m_new)
        p = jnp.exp2((s - jnp.concatenate([m_new] * (width // LANES), axis=1)).astype(BF16))
        vv = v_ref[pl.ds(pl.multiple_of(key0, wp), width), :]
        vv = jnp.concatenate([vv, _ones_column((width, LANES))], axis=1)
        pv = jnp.dot(p, vv, preferred_element_type=F32)
        alpha_w = jnp.concatenate([alpha] * (acc_sc.shape[1] // LANES), axis=1)
        acc_sc[r0:r1, :] = alpha_w * acc_sc[r0:r1, :] + pv
        m_sc[r0:r1, :] = m_new

    @pl.when(i == 0)
    def _first_block_of_head():
        qk(sa_sc, 0)

    def pair(t, carry):
        qk(sb_sc, (2 * t + 1) * tq)
        softmax_pv(sa_sc, 2 * t * tq)
        lead = tq // 4
        qk(sa_sc, (2 * t + 2) * tq, lead)
        softmax_pv(sb_sc, (2 * t + 1) * tq)
        qk(sa_sc, (2 * t + 2) * tq + lead, tq - lead, c0=lead)
        return carry

    lax.fori_loop(0, i // 2, pair, 0)

    d0 = i * tq

    def finalize(part):
        r0 = part * half
        o = acc_sc[r0:r0 + half, 0:dv] / acc_sc[r0:r0 + half, dv:dv + 1]
        if dual:
            lp = lam_ref[...]
            lam = (jnp.exp(jnp.sum(lp[0:1] * lp[1:2], axis=1, keepdims=True))
                   - jnp.exp(jnp.sum(lp[2:3] * lp[3:4], axis=1, keepdims=True))
                   + lambda_init)
            oa = o[0:wp] - lam * o[wp:]
            o = _rms(oa, subln_ref[...]) * (1.0 - lambda_init)
        o_ref[part * wp:(part + 1) * wp, :] = o.astype(BF16)

    def diag_and_finish(s_ref, prefetch):
        early = prefetch and s_ref is not sa_sc
        softmax_pv(s_ref, d0, tq, half, rows, diag=True)
        if early:
            qk_next(wp, tq)
        finalize(1)
        softmax_pv(s_ref, d0, wp, 0, half, diag=True)
        if prefetch:
            qk_next(0, wp if early else tq)
        finalize(0)

    def even_tail(prefetch):
        diag_and_finish(sa_sc, prefetch)

    def odd_tail(prefetch):
        qk(sb_sc, d0, wp, 0, half)
        qk(sb_sc, d0, tq, half, rows)
        softmax_pv(sa_sc, d0 - tq)
        diag_and_finish(sb_sc, prefetch)

    last = nq - 1
    last_tail, other_tail = (even_tail, odd_tail) if last % 2 == 0 else (odd_tail, even_tail)
    pl.when(i == last)(functools.partial(last_tail, False))
    pl.when((i % 2 == last % 2) & (i != last))(functools.partial(last_tail, True))
    pl.when(i % 2 != last % 2)(functools.partial(other_tail, True))


def _flash(q, k, v, extras, *, blk, dual, lambda_init=0.0):
    bsz, heads, seq, dk = q.shape
    dv = v.shape[-1]
    dvp = dv + LANES
    assert blk % CHUNK == 0 and blk & (blk - 1) == 0 and seq % blk == 0
    rows = 2 * blk if dual else blk
    nq = seq // blk
    qspec = pl.BlockSpec((None, None, blk, dk), lambda b, h, i: (b, h, i, 0))
    qnspec = pl.BlockSpec((None, None, blk, dk),
                          lambda b, h, i: (b, h, jnp.minimum(i + 1, nq - 1), 0))
    kspec = pl.BlockSpec((None, None, seq, dk), lambda b, h, i: (b, h, 0, 0))
    vspec = pl.BlockSpec((None, None, seq, dv), lambda b, h, i: (b, h, 0, 0))
    ospec = pl.BlockSpec((None, blk, dv), lambda b, h, i: (b, i, h))
    wp = blk // 2
    frame_chunk = jnp.arange(wp, dtype=jnp.int32) // CHUNK
    tri = (frame_chunk[None, :] <= frame_chunk[:, None]).astype(F32)
    extras = [tri] + list(extras)
    espec = [pl.BlockSpec(e.shape, lambda b, h, i: (0, 0)) for e in extras]
    scratch = [pltpu.VMEM((rows, blk), F32), pltpu.VMEM((rows, blk), F32),
               pltpu.VMEM((rows, LANES), F32), pltpu.VMEM((rows, dvp), F32)]
    if dual:
        scratch = [pltpu.VMEM((rows, dk), BF16)] + scratch
    kern = functools.partial(_flash_kernel, tq=blk, nq=nq, dual=dual, lambda_init=lambda_init)
    return pl.pallas_call(
        kern,
        grid=(bsz, heads, nq),
        in_specs=[qspec, qnspec, kspec, vspec] + espec,
        out_specs=ospec,
        scratch_shapes=scratch,
        out_shape=jax.ShapeDtypeStruct((bsz, seq, heads * dv), BF16),
        compiler_params=pltpu.CompilerParams(
            dimension_semantics=("parallel", "parallel", "arbitrary"),
            vmem_limit_bytes=VMEM_LIMIT),
        name="flash_da" if dual else "flash_mla",
    )(q, q, k, v, *extras)


def _back_kernel(oa_ref, ob_ref, g_ref, x_ref, wa_ref, wb_ref, wo_ref, lnm_ref,
                 lnp_ref, wg_ref, wu_ref, wd_ref, lno_ref, o_ref, *, d, ff_chunk):
    d_ff = wg_ref.shape[1]
    tm = x_ref.shape[0]
    groups = [slice(r, r + tm // 2) for r in range(0, tm, tm // 2)]

    def mix(rs):
        y_a = jnp.dot(oa_ref[rs, :], wa_ref[...], preferred_element_type=F32)
        y_b = jnp.dot(ob_ref[rs, :], wb_ref[...], preferred_element_type=F32)
        merged = g_ref[rs, 0:d].astype(F32) * y_a + g_ref[rs, d:2 * d].astype(F32) * y_b
        return jnp.dot(merged.astype(BF16), wo_ref[...], preferred_element_type=F32)

    def ffn_chunk(h, c):
        w = min(ff_chunk, d_ff - c)
        gate = jnp.dot(h, wg_ref[:, c:c + w], preferred_element_type=F32)
        up = jnp.dot(h, wu_ref[:, c:c + w], preferred_element_type=F32)
        a = (jax.nn.silu(gate) * up).astype(BF16)
        return jnp.dot(a, wd_ref[c:c + w, :], preferred_element_type=F32)

    m = [mix(rs) for rs in groups]
    x1 = [x_ref[rs, :] + _rms(mg, lnm_ref[...]) for rs, mg in zip(groups, m)]
    h = [_rms(xg, lnp_ref[...]).astype(BF16) for xg in x1]
    f = [None] * len(groups)
    for c in range(0, d_ff, ff_chunk):
        for gi in range(len(groups)):
            part = ffn_chunk(h[gi], c)
            f[gi] = part if f[gi] is None else f[gi] + part
    for rs, xg, fg in zip(groups, x1, f):
        o_ref[rs, :] = xg + _rms(fg, lno_ref[...])


def _back(oa, ob, g, x2, wa, wb, wo, lnm, lnp, wg, wu, wd, lno, *, tm):
    t, d = x2.shape
    ff_chunk = 1024
    row = lambda w: pl.BlockSpec((tm, w), lambda i: (i, 0))
    weights = [wa, wb, wo, lnm, lnp, wg, wu, wd, lno]
    return pl.pallas_call(
        functools.partial(_back_kernel, d=d, ff_chunk=ff_chunk),
        grid=(t // tm,),
        in_specs=[row(d), row(d), row(2 * d), row(d)] + [_resident(w.shape) for w in weights],
        out_specs=row(d),
        out_shape=jax.ShapeDtypeStruct((t, d), F32),
        compiler_params=pltpu.CompilerParams(
            dimension_semantics=("parallel",), vmem_limit_bytes=VMEM_LIMIT),
        name="back",
    )(oa, ob, g, x2, *weights)


def _split_w_in(w, d):
    lat = MLA_Q_RANK + MLA_KV_RANK + ROPE_DIM
    zm_w = -(-lat // LANES) * LANES
    latent = jnp.pad(w[:, 3 * d:3 * d + lat].astype(BF16), ((0, 0), (0, zm_w - lat)))
    return [w[:, :3 * d].astype(BF16), latent, w[:, 3 * d + lat:].astype(BF16)]


def _pack_w_uq(w):
    r = w.shape[0]
    w = w.reshape(r, MLA_HEADS, MLA_QK_DIM)
    w = jnp.pad(w, ((0, 0), (0, 0), (0, MLA_HEAD_PAD - MLA_QK_DIM)))
    return w.reshape(r, MLA_HEADS * MLA_HEAD_PAD).astype(BF16)


def kernel(x, positions, ln_mix_pre, w_in, lambda_q1, lambda_k1, lambda_q2, lambda_k2, da_subln,
           q_a_norm, w_uq, kv_a_norm, w_ukv, w_proj_a, w_proj_b, w_o, ln_mix_post, ln_ffn_pre,
           w_ffn_gate, w_ffn_up, w_ffn_down, ln_ffn_post):
    bsz, seq, d = x.shape
    t = bsz * seq
    depth = w_in.shape[0]
    tm = 512

    half = ROPE_DIM // 2
    inv_freq = 1.0 / (ROPE_THETA ** (jnp.arange(half, dtype=F32) * (2.0 / ROPE_DIM)))
    inv_freq_lanes = jnp.tile(inv_freq, LANES // half).reshape(1, LANES)
    pos_lanes = jnp.broadcast_to(positions.reshape(t, 1), (t, LANES))

    x2 = x.reshape(t, d)
    for l in range(depth):
        lambda_init = 0.8 - 0.6 * math.exp(-0.3 * l)
        q_da, k_da, v_da, gates, q_b, k_b, v_b = _front(
            x2, pos_lanes, inv_freq_lanes, ln_mix_pre[l].reshape(1, d), _split_w_in(w_in[l], d),
            q_a_norm[l].reshape(1, -1), kv_a_norm[l].reshape(1, -1),
            _pack_w_uq(w_uq[l]), w_ukv[l].astype(BF16),
            bsz=bsz, seq=seq, d=d, tm=tm,
            qa_scale=DA_HEAD_DIM ** -0.5 * LOG2E, qb_scale=MLA_QK_DIM ** -0.5 * LOG2E)

        lam_p = jnp.stack([lambda_q1[l], lambda_k1[l], lambda_q2[l], lambda_k2[l]]).astype(F32)
        lam_p = jnp.pad(lam_p, ((0, 0), (0, LANES - DA_HEAD_DIM)))
        oa = _flash(q_da, k_da, v_da, [lam_p, da_subln[l].reshape(1, -1)], blk=1024, dual=True,
                    lambda_init=lambda_init)
        ob = _flash(q_b, k_b, v_b, [], blk=1024, dual=False)

        x2 = _back(oa.reshape(t, d), ob.reshape(t, d), gates, x2,
                   w_proj_a[l].astype(BF16), w_proj_b[l].astype(BF16), w_o[l].astype(BF16),
                   ln_mix_post[l].reshape(1, d), ln_ffn_pre[l].reshape(1, d),
                   w_ffn_gate[l].astype(BF16), w_ffn_up[l].astype(BF16),
                   w_ffn_down[l].astype(BF16), ln_ffn_post[l].reshape(1, d), tm=tm)
    return x2.reshape(bsz, seq, d)
```

```python
import functools
import math

import jax
import jax.numpy as jnp
from jax import lax
from jax.experimental import pallas as pl
from jax.experimental.pallas import tpu as pltpu

F32 = jnp.float32
BF16 = jnp.bfloat16

LANES = 128
CHUNK = 64
RMS_EPS = 1e-6
ROPE_THETA = 10000.0
ROPE_DIM = 64
MASK_VALUE = -1e30
LOG2E = math.log2(math.e)

DA_HEAD_DIM = 64
MLA_HEADS = 8
MLA_Q_RANK = 384
MLA_KV_RANK = 256
MLA_NOPE_DIM = 128
MLA_V_DIM = 128
MLA_QK_DIM = MLA_NOPE_DIM + ROPE_DIM
MLA_HEAD_PAD = 256

VMEM_LIMIT = 56 * 1024 * 1024


def _rms(x, w, eps=RMS_EPS):
    return x * lax.rsqrt(jnp.mean(x * x, axis=-1, keepdims=True) + eps) * w


def _rope_tile(x, cos, sin_signed, first_half):
    rot = jnp.where(first_half, pltpu.roll(x, 96, 1), pltpu.roll(x, 32, 1))
    return x * cos + rot * sin_signed


def _first_half_mask(shape):
    lane = lax.broadcasted_iota(jnp.int32, shape, 1)
    return (lane & (ROPE_DIM // 2)) == 0


def _ones_column(shape):
    lane = lax.broadcasted_iota(jnp.int32, shape, 1)
    return jnp.where(lane == 0, 1.0, 0.0).astype(BF16)


def _resident(shape):
    return pl.BlockSpec(shape, lambda *_: (0,) * len(shape), pipeline_mode=pl.Buffered(1))


def _front_kernel(x_ref, pos_ref, invf_ref, ln_ref, wqkv_ref, wlat_ref, wgate_ref,
                  qn_ref, kvn_ref, wq_ref, wkv_ref,
                  qa_ref, ka_ref, va_ref, g_ref, qb_ref, kb_ref, vb_ref,
                  *, d, nc, qa_scale, qb_scale):
    h = _rms(x_ref[...], ln_ref[...]).astype(BF16)
    ang = pos_ref[...].astype(F32) * invf_ref[...]
    first = _first_half_mask(ang.shape)
    cos = jnp.cos(ang)
    sin_plain = jnp.sin(ang)
    sin = jnp.where(first, -sin_plain, sin_plain)

    def mm(c0, width, w_ref=wqkv_ref):
        return jnp.dot(h, w_ref[:, c0:c0 + width], preferred_element_type=F32)

    for c in range(0, d, nc):
        z = mm(c, nc)
        for g in range(0, nc, LANES):
            r = _rope_tile(z[:, g:g + LANES], cos, sin, first) * qa_scale
            qa_ref[(c + g) // LANES] = r.astype(BF16)
    for c in range(0, d, nc):
        z = mm(d + c, nc)
        for g in range(0, nc, LANES):
            r = _rope_tile(z[:, g:g + LANES], cos, sin, first)
            ka_ref[(c + g) // LANES] = r.astype(BF16)
    for c in range(0, d, nc):
        z = mm(2 * d + c, nc)
        for g in range(0, nc, LANES):
            va_ref[(c + g) // LANES] = z[:, g:g + LANES].astype(BF16)
    for c in range(0, 2 * d, nc):
        g_ref[:, c:c + nc] = jax.nn.sigmoid(mm(c, nc, wgate_ref)).astype(BF16)

    zm = mm(0, wlat_ref.shape[1], wlat_ref)
    cq = _rms(zm[:, 0:MLA_Q_RANK], qn_ref[...]).astype(BF16)
    ckv = _rms(zm[:, MLA_Q_RANK:MLA_Q_RANK + MLA_KV_RANK], kvn_ref[...]).astype(BF16)
    kr0 = MLA_Q_RANK + MLA_KV_RANK
    k_rope = _rope_tile(zm[:, kr0:kr0 + LANES], cos, sin, first).astype(BF16)
    for hd in range(MLA_HEADS):
        c = hd * MLA_HEAD_PAD
        qh = jnp.dot(cq, wq_ref[:, c:c + MLA_HEAD_PAD], preferred_element_type=F32)
        qb_ref[hd, :, 0:LANES] = (qh[:, :LANES] * qb_scale).astype(BF16)
        qb_ref[hd, :, LANES:2 * LANES] = (
            _rope_tile(qh[:, LANES:], cos, sin, first) * qb_scale).astype(BF16)
        kvh = jnp.dot(ckv, wkv_ref[:, c:c + MLA_HEAD_PAD], preferred_element_type=F32)
        kb_ref[hd, :, 0:LANES] = kvh[:, :LANES].astype(BF16)
        kb_ref[hd, :, LANES:2 * LANES] = k_rope
        vb_ref[hd] = kvh[:, LANES:].astype(BF16)


def _head_major_spec(heads, tm, w, seq):
    per_seq = seq // tm
    return pl.BlockSpec((None, heads, tm, w), lambda i: (i // per_seq, 0, i % per_seq, 0))


def _front(x2, pos_lanes, inv_freq_lanes, ln, w_parts, qn, kvn, wq_p, wkv,
           *, bsz, seq, d, tm, qa_scale, qb_scale):
    t = x2.shape[0]
    row = lambda w: pl.BlockSpec((tm, w), lambda i: (i, 0))
    da_heads = d // LANES
    hm = lambda heads, w: _head_major_spec(heads, tm, w, seq)
    hm_shape = lambda heads, w: jax.ShapeDtypeStruct((bsz, heads, seq, w), BF16)
    kern = functools.partial(_front_kernel, d=d, nc=512, qa_scale=qa_scale, qb_scale=qb_scale)
    return pl.pallas_call(
        kern,
        grid=(t // tm,),
        in_specs=[row(d), row(LANES), _resident((1, LANES)), _resident((1, d))]
        + [_resident(w.shape) for w in w_parts]
        + [_resident(qn.shape), _resident(kvn.shape), _resident(wq_p.shape),
           _resident(wkv.shape)],
        out_specs=[hm(da_heads, LANES), hm(da_heads, LANES), hm(da_heads, LANES), row(2 * d),
                   hm(MLA_HEADS, MLA_HEAD_PAD), hm(MLA_HEADS, MLA_HEAD_PAD),
                   hm(MLA_HEADS, MLA_V_DIM)],
        out_shape=[hm_shape(da_heads, LANES)] * 3 + [jax.ShapeDtypeStruct((t, 2 * d), BF16)]
        + [hm_shape(MLA_HEADS, MLA_HEAD_PAD)] * 2 + [hm_shape(MLA_HEADS, MLA_V_DIM)],
        compiler_params=pltpu.CompilerParams(
            dimension_semantics=("parallel",), vmem_limit_bytes=VMEM_LIMIT),
        name="front",
    )(x2, pos_lanes, inv_freq_lanes, ln, *w_parts, qn, kvn, wq_p, wkv)


def _flash_kernel(q_ref, qn_ref, k_ref, v_ref, tri_ref, *rest, tq, nq, dual, lambda_init):
    if dual:
        lam_ref, subln_ref, o_ref, qz_sc, sa_sc, sb_sc, m_sc, acc_sc = rest
    else:
        o_ref, sa_sc, sb_sc, m_sc, acc_sc = rest
    i = pl.program_id(2)
    dv = o_ref.shape[-1]
    rows = m_sc.shape[0]
    half = rows // 2
    wp = tq // 2
    nt = (((1,), (1,)), ((), ()))

    def stack(q_in):
        lo = lax.broadcasted_iota(jnp.int32, q_in.shape, 1) < DA_HEAD_DIM
        zero = jnp.zeros_like(q_in)
        q1 = jnp.where(lo, q_in, zero)
        q2 = jnp.where(lo, zero, q_in)
        return jnp.concatenate([q1[0:wp], q2[0:wp], q1[wp:], q2[wp:]], axis=0)

    if dual:
        qz_sc[...] = stack(q_ref[...])
    m_sc[...] = jnp.full(m_sc.shape, -jnp.inf, F32)
    acc_sc[...] = jnp.zeros(acc_sc.shape, F32)
    q_src = qz_sc if dual else q_ref

    def qk(s_ref, key0, width=tq, r0=0, r1=rows, c0=0):
        kk = k_ref[pl.ds(pl.multiple_of(key0, wp), width), :]
        s_ref[r0:r1, c0:c0 + width] = lax.dot_general(
            q_src[r0:r1, :], kk, nt, preferred_element_type=F32)

    def qk_next(c0=0, c1=tq):
        qn = stack(qn_ref[...]) if dual else qn_ref[...]
        sa_sc[:, c0:c1] = lax.dot_general(qn, k_ref[c0:c1, :], nt, preferred_element_type=F32)

    def chunk_causal(s):
        allowed = tri_ref[...] != 0.0
        groups = [jnp.where(allowed, s[g:g + wp], MASK_VALUE) for g in range(0, s.shape[0], wp)]
        return jnp.concatenate(groups, axis=0)

    def softmax_pv(s_ref, key0, width=tq, r0=0, r1=rows, diag=False):
        s = s_ref[r0:r1, 0:width]
        if diag:
            tail = chunk_causal(s[:, width - wp:])
            s = tail if width == wp else jnp.concatenate([s[:, 0:width - wp], tail], axis=1)
        m_prev = m_sc[r0:r1, :]
        m_new = jnp.maximum(m_prev, jnp.max(s, axis=1, keepdims=True))
        alpha = jnp.exp2(m_prev - m_new)
        p = jnp.exp2((s - jnp.concatenate([m_new] * (width // LANES), axis=1)).astype(BF16))
        vv = v_ref[pl.ds(pl.multiple_of(key0, wp), width), :]
        vv = jnp.concatenate([vv, _ones_column((width, LANES))], axis=1)
        pv = jnp.dot(p, vv, preferred_element_type=F32)
        alpha_w = jnp.concatenate([alpha] * (acc_sc.shape[1] // LANES), axis=1)
        acc_sc[r0:r1, :] = alpha_w * acc_sc[r0:r1, :] + pv
        m_sc[r0:r1, :] = m_new

    @pl.when(i == 0)
    def _first_block_of_head():
        qk(sa_sc, 0)

    def pair(t, carry):
        qk(sb_sc, (2 * t + 1) * tq)
        softmax_pv(sa_sc, 2 * t * tq)
        lead = tq // 4
        qk(sa_sc, (2 * t + 2) * tq, lead)
        softmax_pv(sb_sc, (2 * t + 1) * tq)
        qk(sa_sc, (2 * t + 2) * tq + lead, tq - lead, c0=lead)
        return carry

    lax.fori_loop(0, i // 2, pair, 0)

    d0 = i * tq

    def finalize(part):
        r0 = part * half
        o = acc_sc[r0:r0 + half, 0:dv] / acc_sc[r0:r0 + half, dv:dv + 1]
        if dual:
            lp = lam_ref[...]
            lam = (jnp.exp(jnp.sum(lp[0:1] * lp[1:2], axis=1, keepdims=True))
                   - jnp.exp(jnp.sum(lp[2:3] * lp[3:4], axis=1, keepdims=True))
                   + lambda_init)
            oa = o[0:wp] - lam * o[wp:]
            o = _rms(oa, subln_ref[...]) * (1.0 - lambda_init)
        o_ref[part * wp:(part + 1) * wp, :] = o.astype(BF16)

    def diag_and_finish(s_ref, prefetch):
        early = prefetch and s_ref is not sa_sc
        softmax_pv(s_ref, d0, tq, half, rows, diag=True)
        if early:
            qk_next(wp, tq)
        finalize(1)
        softmax_pv(s_ref, d0, wp, 0, half, diag=True)
        if prefetch:
            qk_next(0, wp if early else tq)
        finalize(0)

    def even_tail(prefetch):
        diag_and_finish(sa_sc, prefetch)

    def odd_tail(prefetch):
        qk(sb_sc, d0, wp, 0, half)
        qk(sb_sc, d0, tq, half, rows)
        softmax_pv(sa_sc, d0 - tq)
        diag_and_finish(sb_sc, prefetch)

    last = nq - 1
    last_tail, other_tail = (even_tail, odd_tail) if last % 2 == 0 else (odd_tail, even_tail)
    pl.when(i == last)(functools.partial(last_tail, False))
    pl.when((i % 2 == last % 2) & (i != last))(functools.partial(last_tail, True))
    pl.when(i % 2 != last % 2)(functools.partial(other_tail, True))


def _flash(q, k, v, extras, *, blk, dual, lambda_init=0.0):
    bsz, heads, seq, dk = q.shape
    dv = v.shape[-1]
    dvp = dv + LANES
    assert blk % CHUNK == 0 and blk & (blk - 1) == 0 and seq % blk == 0
    rows = 2 * blk if dual else blk
    nq = seq // blk
    qspec = pl.BlockSpec((None, None, blk, dk), lambda b, h, i: (b, h, i, 0))
    qnspec = pl.BlockSpec((None, None, blk, dk),
                          lambda b, h, i: (b, h, jnp.minimum(i + 1, nq - 1), 0))
    kspec = pl.BlockSpec((None, None, seq, dk), lambda b, h, i: (b, h, 0, 0))
    vspec = pl.BlockSpec((None, None, seq, dv), lambda b, h, i: (b, h, 0, 0))
    ospec = pl.BlockSpec((None, blk, dv), lambda b, h, i: (b, i, h))
    wp = blk // 2
    frame_chunk = jnp.arange(wp, dtype=jnp.int32) // CHUNK
    tri = (frame_chunk[None, :] <= frame_chunk[:, None]).astype(F32)
    extras = [tri] + list(extras)
    espec = [pl.BlockSpec(e.shape, lambda b, h, i: (0, 0)) for e in extras]
    scratch = [pltpu.VMEM((rows, blk), F32), pltpu.VMEM((rows, blk), F32),
               pltpu.VMEM((rows, LANES), F32), pltpu.VMEM((rows, dvp), F32)]
    if dual:
        scratch = [pltpu.VMEM((rows, dk), BF16)] + scratch
    kern = functools.partial(_flash_kernel, tq=blk, nq=nq, dual=dual, lambda_init=lambda_init)
    return pl.pallas_call(
        kern,
        grid=(bsz, heads, nq),
        in_specs=[qspec, qnspec, kspec, vspec] + espec,
        out_specs=ospec,
        scratch_shapes=scratch,
        out_shape=jax.ShapeDtypeStruct((bsz, seq, heads * dv), BF16),
        compiler_params=pltpu.CompilerParams(
            dimension_semantics=("parallel", "parallel", "arbitrary"),
            vmem_limit_bytes=VMEM_LIMIT),
        name="flash_da" if dual else "flash_mla",
    )(q, q, k, v, *extras)


def _flash_head_kernel(q_ref, k_ref, v_ref, tri_ref, *rest, tq, nq, dual, lambda_init):
    if dual:
        lam_ref, subln_ref, o_ref, qz_sc, sa_sc, sb_sc, m_sc, acc_sc = rest
    else:
        o_ref, qz_sc, sa_sc, sb_sc, m_sc, acc_sc = rest
    dv = o_ref.shape[-1]
    rows = m_sc.shape[0]
    half = rows // 2
    wp = tq // 2
    nt = (((1,), (1,)), ((), ()))

    def stack(q_in):
        if not dual:
            return q_in
        lo = lax.broadcasted_iota(jnp.int32, q_in.shape, 1) < DA_HEAD_DIM
        zero = jnp.zeros_like(q_in)
        q1 = jnp.where(lo, q_in, zero)
        q2 = jnp.where(lo, zero, q_in)
        return jnp.concatenate([q1[0:wp], q2[0:wp], q1[wp:], q2[wp:]], axis=0)

    def q_block(qi):
        return stack(q_ref[pl.ds(pl.multiple_of(qi * tq, tq), tq), :])

    def qk(s_ref, key0, width=tq, r0=0, r1=rows, c0=0):
        kk = k_ref[pl.ds(pl.multiple_of(key0, wp), width), :]
        s_ref[r0:r1, c0:c0 + width] = lax.dot_general(
            qz_sc[r0:r1, :], kk, nt, preferred_element_type=F32)

    def qk_first(qi, c0=0, c1=tq):
        sa_sc[:, c0:c1] = lax.dot_general(q_block(qi), k_ref[c0:c1, :], nt,
                                          preferred_element_type=F32)

    def chunk_causal(s):
        allowed = tri_ref[...] != 0.0
        groups = [jnp.where(allowed, s[g:g + wp], MASK_VALUE) for g in range(0, s.shape[0], wp)]
        return jnp.concatenate(groups, axis=0)

    def softmax_pv(s_ref, key0, width=tq, r0=0, r1=rows, diag=False):
        s = s_ref[r0:r1, 0:width]
        if diag:
            tail = chunk_causal(s[:, width - wp:])
            s = tail if width == wp else jnp.concatenate([s[:, 0:width - wp], tail], axis=1)
        m_prev = m_sc[r0:r1, :]
        m_new = jnp.maximum(m_prev, jnp.max(s, axis=1, keepdims=True))
        alpha = jnp.exp2(m_prev - m_new)
        p = jnp.exp2((s - jnp.concatenate([m_new] * (width // LANES), axis=1)).astype(BF16))
        vv = v_ref[pl.ds(pl.multiple_of(key0, wp), width), :]
        vv = jnp.concatenate([vv, _ones_column((width, LANES))], axis=1)
        pv = jnp.dot(p, vv, preferred_element_type=F32)
        alpha_w = jnp.concatenate([alpha] * (acc_sc.shape[1] // LANES), axis=1)
        acc_sc[r0:r1, :] = alpha_w * acc_sc[r0:r1, :] + pv
        m_sc[r0:r1, :] = m_new

    def step(i, odd):
        qz_sc[...] = q_block(i)
        m_sc[...] = jnp.full(m_sc.shape, -jnp.inf, F32)
        acc_sc[...] = jnp.zeros(acc_sc.shape, F32)
        nxt = jnp.minimum(i + 1, nq - 1)

        def pair(t, carry):
            qk(sb_sc, (2 * t + 1) * tq)
            softmax_pv(sa_sc, 2 * t * tq)
            lead = tq // 4
            qk(sa_sc, (2 * t + 2) * tq, lead)
            softmax_pv(sb_sc, (2 * t + 1) * tq)
            qk(sa_sc, (2 * t + 2) * tq + lead, tq - lead, c0=lead)
            return carry

        lax.fori_loop(0, i // 2, pair, 0)
        d0 = i * tq

        def finalize(part):
            r0 = part * half
            o = acc_sc[r0:r0 + half, 0:dv] / acc_sc[r0:r0 + half, dv:dv + 1]
            if dual:
                lp = lam_ref[...]
                lam = (jnp.exp(jnp.sum(lp[0:1] * lp[1:2], axis=1, keepdims=True))
                       - jnp.exp(jnp.sum(lp[2:3] * lp[3:4], axis=1, keepdims=True))
                       + lambda_init)
                oa = o[0:wp] - lam * o[wp:]
                o = _rms(oa, subln_ref[...]) * (1.0 - lambda_init)
            o_ref[pl.ds(pl.multiple_of(d0 + part * wp, wp), wp), :] = o.astype(BF16)

        s_ref = sb_sc if odd else sa_sc
        if odd:
            qk(sb_sc, d0, wp, 0, half)
            qk(sb_sc, d0, tq, half, rows)
            softmax_pv(sa_sc, d0 - tq)
        softmax_pv(s_ref, d0, tq, half, rows, diag=True)
        if odd:
            qk_first(nxt, wp, tq)
        finalize(1)
        softmax_pv(s_ref, d0, wp, 0, half, diag=True)
        qk_first(nxt, 0, wp if odd else tq)
        finalize(0)

    qk_first(0)

    def two_blocks(u, carry):
        step(2 * u, False)
        step(2 * u + 1, True)
        return carry

    lax.fori_loop(0, nq // 2, two_blocks, 0)


def _flash_heads(q, k, v, extras, *, blk, dual, lambda_init=0.0):
    bsz, heads, seq, dk = q.shape
    dv = v.shape[-1]
    assert blk & (blk - 1) == 0 and seq % (2 * blk) == 0
    rows = 2 * blk if dual else blk
    nq = seq // blk
    wp = blk // 2
    frame_chunk = jnp.arange(wp, dtype=jnp.int32) // CHUNK
    tri = (frame_chunk[None, :] <= frame_chunk[:, None]).astype(F32)
    extras = [tri] + list(extras)
    whole = lambda w: pl.BlockSpec((None, None, seq, w), lambda b, h: (b, h, 0, 0))
    scratch = [pltpu.VMEM((rows, dk), BF16),
               pltpu.VMEM((rows, blk), F32), pltpu.VMEM((rows, blk), F32),
               pltpu.VMEM((rows, LANES), F32), pltpu.VMEM((rows, dv + LANES), F32)]
    kern = functools.partial(_flash_head_kernel, tq=blk, nq=nq, dual=dual, lambda_init=lambda_init)
    return pl.pallas_call(
        kern,
        grid=(bsz, heads),
        in_specs=[whole(dk), whole(dk), whole(dv)]
        + [pl.BlockSpec(e.shape, lambda b, h: (0, 0)) for e in extras],
        out_specs=pl.BlockSpec((None, seq, dv), lambda b, h: (b, 0, h)),
        scratch_shapes=scratch,
        out_shape=jax.ShapeDtypeStruct((bsz, seq, heads * dv), BF16),
        compiler_params=pltpu.CompilerParams(
            dimension_semantics=("parallel", "parallel"), vmem_limit_bytes=VMEM_LIMIT),
        name="flash_da" if dual else "flash_mla",
    )(q, k, v, *extras)


def _back_kernel(oa_ref, ob_ref, g_ref, x_ref, wa_ref, wb_ref, wo_ref, lnm_ref,
                 lnp_ref, wg_ref, wu_ref, wd_ref, lno_ref, o_ref, *, d, ff_chunk):
    d_ff = wg_ref.shape[1]
    tm = x_ref.shape[0]
    groups = [slice(r, r + tm // 2) for r in range(0, tm, tm // 2)]

    def mix(rs):
        y_a = jnp.dot(oa_ref[rs, :], wa_ref[...], preferred_element_type=F32)
        y_b = jnp.dot(ob_ref[rs, :], wb_ref[...], preferred_element_type=F32)
        merged = g_ref[rs, 0:d].astype(F32) * y_a + g_ref[rs, d:2 * d].astype(F32) * y_b
        return jnp.dot(merged.astype(BF16), wo_ref[...], preferred_element_type=F32)

    def ffn_chunk(h, c):
        w = min(ff_chunk, d_ff - c)
        gate = jnp.dot(h, wg_ref[:, c:c + w], preferred_element_type=F32)
        up = jnp.dot(h, wu_ref[:, c:c + w], preferred_element_type=F32)
        a = (jax.nn.silu(gate) * up).astype(BF16)
        return jnp.dot(a, wd_ref[c:c + w, :], preferred_element_type=F32)

    m = [mix(rs) for rs in groups]
    x1 = [x_ref[rs, :] + _rms(mg, lnm_ref[...]) for rs, mg in zip(groups, m)]
    h = [_rms(xg, lnp_ref[...]).astype(BF16) for xg in x1]
    f = [None] * len(groups)
    for c in range(0, d_ff, ff_chunk):
        for gi in range(len(groups)):
            part = ffn_chunk(h[gi], c)
            f[gi] = part if f[gi] is None else f[gi] + part
    for rs, xg, fg in zip(groups, x1, f):
        o_ref[rs, :] = xg + _rms(fg, lno_ref[...])


def _back(oa, ob, g, x2, wa, wb, wo, lnm, lnp, wg, wu, wd, lno, *, tm):
    t, d = x2.shape
    ff_chunk = 1024
    row = lambda w: pl.BlockSpec((tm, w), lambda i: (i, 0))
    weights = [wa, wb, wo, lnm, lnp, wg, wu, wd, lno]
    return pl.pallas_call(
        functools.partial(_back_kernel, d=d, ff_chunk=ff_chunk),
        grid=(t // tm,),
        in_specs=[row(d), row(d), row(2 * d), row(d)] + [_resident(w.shape) for w in weights],
        out_specs=row(d),
        out_shape=jax.ShapeDtypeStruct((t, d), F32),
        compiler_params=pltpu.CompilerParams(
            dimension_semantics=("parallel",), vmem_limit_bytes=VMEM_LIMIT),
        name="back",
    )(oa, ob, g, x2, *weights)


def _split_w_in(w, d):
    lat = MLA_Q_RANK + MLA_KV_RANK + ROPE_DIM
    zm_w = -(-lat // LANES) * LANES
    latent = jnp.pad(w[:, 3 * d:3 * d + lat].astype(BF16), ((0, 0), (0, zm_w - lat)))
    return [w[:, :3 * d].astype(BF16), latent, w[:, 3 * d + lat:].astype(BF16)]


def _pack_w_uq(w):
    r = w.shape[0]
    w = w.reshape(r, MLA_HEADS, MLA_QK_DIM)
    w = jnp.pad(w, ((0, 0), (0, 0), (0, MLA_HEAD_PAD - MLA_QK_DIM)))
    return w.reshape(r, MLA_HEADS * MLA_HEAD_PAD).astype(BF16)


def kernel(x, positions, ln_mix_pre, w_in, lambda_q1, lambda_k1, lambda_q2, lambda_k2, da_subln,
           q_a_norm, w_uq, kv_a_norm, w_ukv, w_proj_a, w_proj_b, w_o, ln_mix_post, ln_ffn_pre,
           w_ffn_gate, w_ffn_up, w_ffn_down, ln_ffn_post):
    bsz, seq, d = x.shape
    t = bsz * seq
    depth = w_in.shape[0]
    tm = 512

    half = ROPE_DIM // 2
    inv_freq = 1.0 / (ROPE_THETA ** (jnp.arange(half, dtype=F32) * (2.0 / ROPE_DIM)))
    inv_freq_lanes = jnp.tile(inv_freq, LANES // half).reshape(1, LANES)
    pos_lanes = jnp.broadcast_to(positions.reshape(t, 1), (t, LANES))

    x2 = x.reshape(t, d)
    for l in range(depth):
        lambda_init = 0.8 - 0.6 * math.exp(-0.3 * l)
        q_da, k_da, v_da, gates, q_b, k_b, v_b = _front(
            x2, pos_lanes, inv_freq_lanes, ln_mix_pre[l].reshape(1, d), _split_w_in(w_in[l], d),
            q_a_norm[l].reshape(1, -1), kv_a_norm[l].reshape(1, -1),
            _pack_w_uq(w_uq[l]), w_ukv[l].astype(BF16),
            bsz=bsz, seq=seq, d=d, tm=tm,
            qa_scale=DA_HEAD_DIM ** -0.5 * LOG2E, qb_scale=MLA_QK_DIM ** -0.5 * LOG2E)

        lam_p = jnp.stack([lambda_q1[l], lambda_k1[l], lambda_q2[l], lambda_k2[l]]).astype(F32)
        lam_p = jnp.pad(lam_p, ((0, 0), (0, LANES - DA_HEAD_DIM)))
        oa = _flash_heads(q_da, k_da, v_da, [lam_p, da_subln[l].reshape(1, -1)], blk=1024,
                          dual=True, lambda_init=lambda_init)
        ob = _flash_heads(q_b, k_b, v_b, [], blk=1024, dual=False)

        x2 = _back(oa.reshape(t, d), ob.reshape(t, d), gates, x2,
                   w_proj_a[l].astype(BF16), w_proj_b[l].astype(BF16), w_o[l].astype(BF16),
                   ln_mix_post[l].reshape(1, d), ln_ffn_pre[l].reshape(1, d),
                   w_ffn_gate[l].astype(BF16), w_ffn_up[l].astype(BF16),
                   w_ffn_down[l].astype(BF16), ln_ffn_post[l].reshape(1, d), tm=tm)
    return x2.reshape(bsz, seq, d)
```

```python
import functools
import math

import jax
import jax.numpy as jnp
from jax import lax
from jax.experimental import pallas as pl
from jax.experimental.pallas import tpu as pltpu

F32 = jnp.float32
BF16 = jnp.bfloat16

LANES = 128
CHUNK = 64
RMS_EPS = 1e-6
ROPE_THETA = 10000.0
ROPE_DIM = 64
MASK_VALUE = -1e30
LOG2E = math.log2(math.e)

DA_HEAD_DIM = 64
MLA_HEADS = 8
MLA_Q_RANK = 384
MLA_KV_RANK = 256
MLA_NOPE_DIM = 128
MLA_V_DIM = 128
MLA_QK_DIM = MLA_NOPE_DIM + ROPE_DIM
MLA_HEAD_PAD = 256

VMEM_LIMIT = 56 * 1024 * 1024


def _rms(x, w, eps=RMS_EPS):
    return x * lax.rsqrt(jnp.mean(x * x, axis=-1, keepdims=True) + eps) * w


def _rope_tile(x, cos, sin_signed, first_half):
    rot = jnp.where(first_half, pltpu.roll(x, 96, 1), pltpu.roll(x, 32, 1))
    return x * cos + rot * sin_signed


def _first_half_mask(shape):
    lane = lax.broadcasted_iota(jnp.int32, shape, 1)
    return (lane & (ROPE_DIM // 2)) == 0


def _ones_column(shape):
    lane = lax.broadcasted_iota(jnp.int32, shape, 1)
    return jnp.where(lane == 0, 1.0, 0.0).astype(BF16)


def _resident(shape):
    return pl.BlockSpec(shape, lambda *_: (0,) * len(shape), pipeline_mode=pl.Buffered(1))


def _front_kernel(x_ref, pos_ref, invf_ref, ln_ref, wqkv_ref, wlat_ref, wgate_ref,
                  qn_ref, kvn_ref, wq_ref, wkv_ref,
                  qa_ref, ka_ref, va_ref, g_ref, qb_ref, kb_ref, vb_ref,
                  *, d, nc, qa_scale, qb_scale):
    h = _rms(x_ref[...], ln_ref[...]).astype(BF16)
    ang = pos_ref[...].astype(F32) * invf_ref[...]
    first = _first_half_mask(ang.shape)
    cos = jnp.cos(ang)
    sin_plain = jnp.sin(ang)
    sin = jnp.where(first, -sin_plain, sin_plain)

    def mm(c0, width, w_ref=wqkv_ref):
        return jnp.dot(h, w_ref[:, c0:c0 + width], preferred_element_type=F32)

    for c in range(0, d, nc):
        z = mm(c, nc)
        for g in range(0, nc, LANES):
            r = _rope_tile(z[:, g:g + LANES], cos, sin, first) * qa_scale
            qa_ref[(c + g) // LANES] = r.astype(BF16)
    for c in range(0, d, nc):
        z = mm(d + c, nc)
        for g in range(0, nc, LANES):
            r = _rope_tile(z[:, g:g + LANES], cos, sin, first)
            ka_ref[(c + g) // LANES] = r.astype(BF16)
    for c in range(0, d, nc):
        z = mm(2 * d + c, nc)
        for g in range(0, nc, LANES):
            va_ref[(c + g) // LANES] = z[:, g:g + LANES].astype(BF16)
    for c in range(0, 2 * d, nc):
        g_ref[:, c:c + nc] = jax.nn.sigmoid(mm(c, nc, wgate_ref)).astype(BF16)

    zm = mm(0, wlat_ref.shape[1], wlat_ref)
    cq = _rms(zm[:, 0:MLA_Q_RANK], qn_ref[...]).astype(BF16)
    ckv = _rms(zm[:, MLA_Q_RANK:MLA_Q_RANK + MLA_KV_RANK], kvn_ref[...]).astype(BF16)
    kr0 = MLA_Q_RANK + MLA_KV_RANK
    k_rope = _rope_tile(zm[:, kr0:kr0 + LANES], cos, sin, first).astype(BF16)
    for hd in range(MLA_HEADS):
        c = hd * MLA_HEAD_PAD
        qh = jnp.dot(cq, wq_ref[:, c:c + MLA_HEAD_PAD], preferred_element_type=F32)
        qb_ref[hd, :, 0:LANES] = (qh[:, :LANES] * qb_scale).astype(BF16)
        qb_ref[hd, :, LANES:2 * LANES] = (
            _rope_tile(qh[:, LANES:], cos, sin, first) * qb_scale).astype(BF16)
        kvh = jnp.dot(ckv, wkv_ref[:, c:c + MLA_HEAD_PAD], preferred_element_type=F32)
        kb_ref[hd, :, 0:LANES] = kvh[:, :LANES].astype(BF16)
        kb_ref[hd, :, LANES:2 * LANES] = k_rope
        vb_ref[hd] = kvh[:, LANES:].astype(BF16)


def _head_major_spec(heads, tm, w, seq):
    per_seq = seq // tm
    return pl.BlockSpec((None, heads, tm, w), lambda i: (i // per_seq, 0, i % per_seq, 0))


def _front(x2, pos_lanes, inv_freq_lanes, ln, w_parts, qn, kvn, wq_p, wkv,
           *, bsz, seq, d, tm, qa_scale, qb_scale):
    t = x2.shape[0]
    row = lambda w: pl.BlockSpec((tm, w), lambda i: (i, 0))
    da_heads = d // LANES
    hm = lambda heads, w: _head_major_spec(heads, tm, w, seq)
    hm_shape = lambda heads, w: jax.ShapeDtypeStruct((bsz, heads, seq, w), BF16)
    kern = functools.partial(_front_kernel, d=d, nc=512, qa_scale=qa_scale, qb_scale=qb_scale)
    return pl.pallas_call(
        kern,
        grid=(t // tm,),
        in_specs=[row(d), row(LANES), _resident((1, LANES)), _resident((1, d))]
        + [_resident(w.shape) for w in w_parts]
        + [_resident(qn.shape), _resident(kvn.shape), _resident(wq_p.shape),
           _resident(wkv.shape)],
        out_specs=[hm(da_heads, LANES), hm(da_heads, LANES), hm(da_heads, LANES), row(2 * d),
                   hm(MLA_HEADS, MLA_HEAD_PAD), hm(MLA_HEADS, MLA_HEAD_PAD),
                   hm(MLA_HEADS, MLA_V_DIM)],
        out_shape=[hm_shape(da_heads, LANES)] * 3 + [jax.ShapeDtypeStruct((t, 2 * d), BF16)]
        + [hm_shape(MLA_HEADS, MLA_HEAD_PAD)] * 2 + [hm_shape(MLA_HEADS, MLA_V_DIM)],
        compiler_params=pltpu.CompilerParams(
            dimension_semantics=("parallel",), vmem_limit_bytes=VMEM_LIMIT),
        name="front",
    )(x2, pos_lanes, inv_freq_lanes, ln, *w_parts, qn, kvn, wq_p, wkv)


def _flash_kernel(q_ref, qn_ref, k_ref, v_ref, tri_ref, *rest, tq, nq, dual, lambda_init):
    if dual:
        lam_ref, subln_ref, o_ref, qz_sc, sa_sc, sb_sc, m_sc, acc_sc = rest
    else:
        o_ref, sa_sc, sb_sc, m_sc, acc_sc = rest
    i = pl.program_id(2)
    dv = o_ref.shape[-1]
    rows = m_sc.shape[0]
    half = rows // 2
    wp = tq // 2
    nt = (((1,), (1,)), ((), ()))

    def stack(q_in):
        lo = lax.broadcasted_iota(jnp.int32, q_in.shape, 1) < DA_HEAD_DIM
        zero = jnp.zeros_like(q_in)
        q1 = jnp.where(lo, q_in, zero)
        q2 = jnp.where(lo, zero, q_in)
        return jnp.concatenate([q1[0:wp], q2[0:wp], q1[wp:], q2[wp:]], axis=0)

    if dual:
        qz_sc[...] = stack(q_ref[...])
    m_sc[...] = jnp.full(m_sc.shape, -jnp.inf, F32)
    acc_sc[...] = jnp.zeros(acc_sc.shape, F32)
    q_src = qz_sc if dual else q_ref

    def qk(s_ref, key0, width=tq, r0=0, r1=rows, c0=0):
        kk = k_ref[pl.ds(pl.multiple_of(key0, wp), width), :]
        s_ref[r0:r1, c0:c0 + width] = lax.dot_general(
            q_src[r0:r1, :], kk, nt, preferred_element_type=F32)

    def qk_next(c0=0, c1=tq):
        qn = stack(qn_ref[...]) if dual else qn_ref[...]
        sa_sc[:, c0:c1] = lax.dot_general(qn, k_ref[c0:c1, :], nt, preferred_element_type=F32)

    def chunk_causal(s):
        allowed = tri_ref[...] != 0.0
        groups = [jnp.where(allowed, s[g:g + wp], MASK_VALUE) for g in range(0, s.shape[0], wp)]
        return jnp.concatenate(groups, axis=0)

    def softmax_pv(s_ref, key0, width=tq, r0=0, r1=rows, diag=False, after=None):
        s = s_ref[r0:r1, 0:width]
        if diag:
            tail = chunk_causal(s[:, width - wp:])
            s = tail if width == wp else jnp.concatenate([s[:, 0:width - wp], tail], axis=1)
        if after is None:
            block_max = jnp.max(s, axis=1, keepdims=True)
        else:
            running = s[:, 0:LANES] + after
            for c in range(LANES, width, LANES):
                running = jnp.maximum(running, s[:, c:c + LANES])
            block_max = jnp.max(running, axis=1, keepdims=True)
        m_prev = m_sc[r0:r1, :]
        m_new = jnp.maximum(m_prev, block_max)
        alpha = jnp.exp2(m_prev - m_new)
        p = jnp.exp2((s - jnp.concatenate([m_new] * (width // LANES), axis=1)).astype(BF16))
        done = jnp.minimum(p[:, width - LANES:].astype(F32), 0.0)
        vv = v_ref[pl.ds(pl.multiple_of(key0, wp), width), :]
        vv = jnp.concatenate([vv, _ones_column((width, LANES))], axis=1)
        pv = jnp.dot(p, vv, preferred_element_type=F32)
        alpha_w = jnp.concatenate([alpha] * (acc_sc.shape[1] // LANES), axis=1)
        acc_sc[r0:r1, :] = alpha_w * acc_sc[r0:r1, :] + pv
        m_sc[r0:r1, :] = m_new
        return done

    @pl.when(i == 0)
    def _first_block_of_head():
        qk(sa_sc, 0)

    def pair(t, carry):
        qk(sb_sc, (2 * t + 1) * tq)
        a_done = softmax_pv(sa_sc, 2 * t * tq)
        lead = tq // 4
        qk(sa_sc, (2 * t + 2) * tq, lead)
        softmax_pv(sb_sc, (2 * t + 1) * tq, after=a_done)
        qk(sa_sc, (2 * t + 2) * tq + lead, tq - lead, c0=lead)
        return carry

    lax.fori_loop(0, i // 2, pair, 0)

    d0 = i * tq

    def finalize(part):
        r0 = part * half
        o = acc_sc[r0:r0 + half, 0:dv] / acc_sc[r0:r0 + half, dv:dv + 1]
        if dual:
            lp = lam_ref[...]
            lam = (jnp.exp(jnp.sum(lp[0:1] * lp[1:2], axis=1, keepdims=True))
                   - jnp.exp(jnp.sum(lp[2:3] * lp[3:4], axis=1, keepdims=True))
                   + lambda_init)
            oa = o[0:wp] - lam * o[wp:]
            o = _rms(oa, subln_ref[...]) * (1.0 - lambda_init)
        o_ref[part * wp:(part + 1) * wp, :] = o.astype(BF16)

    def diag_and_finish(s_ref, prefetch):
        early = prefetch and s_ref is not sa_sc
        softmax_pv(s_ref, d0, tq, half, rows, diag=True)
        if early:
            qk_next(wp, tq)
        finalize(1)
        softmax_pv(s_ref, d0, wp, 0, half, diag=True)
        if prefetch:
            qk_next(0, wp if early else tq)
        finalize(0)

    def even_tail(prefetch):
        diag_and_finish(sa_sc, prefetch)

    def odd_tail(prefetch):
        qk(sb_sc, d0, wp, 0, half)
        qk(sb_sc, d0, tq, half, rows)
        softmax_pv(sa_sc, d0 - tq)
        diag_and_finish(sb_sc, prefetch)

    last = nq - 1
    last_tail, other_tail = (even_tail, odd_tail) if last % 2 == 0 else (odd_tail, even_tail)
    pl.when(i == last)(functools.partial(last_tail, False))
    pl.when((i % 2 == last % 2) & (i != last))(functools.partial(last_tail, True))
    pl.when(i % 2 != last % 2)(functools.partial(other_tail, True))


def _flash(q, k, v, extras, *, blk, dual, lambda_init=0.0):
    bsz, heads, seq, dk = q.shape
    dv = v.shape[-1]
    dvp = dv + LANES
    assert blk % CHUNK == 0 and blk & (blk - 1) == 0 and seq % blk == 0
    rows = 2 * blk if dual else blk
    nq = seq // blk
    qspec = pl.BlockSpec((None, None, blk, dk), lambda b, h, i: (b, h, i, 0))
    qnspec = pl.BlockSpec((None, None, blk, dk),
                          lambda b, h, i: (b, h, jnp.minimum(i + 1, nq - 1), 0))
    kspec = pl.BlockSpec((None, None, seq, dk), lambda b, h, i: (b, h, 0, 0))
    vspec = pl.BlockSpec((None, None, seq, dv), lambda b, h, i: (b, h, 0, 0))
    ospec = pl.BlockSpec((None, blk, dv), lambda b, h, i: (b, i, h))
    wp = blk // 2
    frame_chunk = jnp.arange(wp, dtype=jnp.int32) // CHUNK
    tri = (frame_chunk[None, :] <= frame_chunk[:, None]).astype(F32)
    extras = [tri] + list(extras)
    espec = [pl.BlockSpec(e.shape, lambda b, h, i: (0, 0)) for e in extras]
    scratch = [pltpu.VMEM((rows, blk), F32), pltpu.VMEM((rows, blk), F32),
               pltpu.VMEM((rows, LANES), F32), pltpu.VMEM((rows, dvp), F32)]
    if dual:
        scratch = [pltpu.VMEM((rows, dk), BF16)] + scratch
    kern = functools.partial(_flash_kernel, tq=blk, nq=nq, dual=dual, lambda_init=lambda_init)
    return pl.pallas_call(
        kern,
        grid=(bsz, heads, nq),
        in_specs=[qspec, qnspec, kspec, vspec] + espec,
        out_specs=ospec,
        scratch_shapes=scratch,
        out_shape=jax.ShapeDtypeStruct((bsz, seq, heads * dv), BF16),
        compiler_params=pltpu.CompilerParams(
            dimension_semantics=("parallel", "parallel", "arbitrary"),
            vmem_limit_bytes=VMEM_LIMIT),
        name="flash_da" if dual else "flash_mla",
    )(q, q, k, v, *extras)


def _back_kernel(oa_ref, ob_ref, g_ref, x_ref, wa_ref, wb_ref, wo_ref, lnm_ref,
                 lnp_ref, wg_ref, wu_ref, wd_ref, lno_ref, o_ref, *, d, ff_chunk):
    d_ff = wg_ref.shape[1]
    tm = x_ref.shape[0]
    groups = [slice(r, r + tm // 2) for r in range(0, tm, tm // 2)]

    def mix(rs):
        y_a = jnp.dot(oa_ref[rs, :], wa_ref[...], preferred_element_type=F32)
        y_b = jnp.dot(ob_ref[rs, :], wb_ref[...], preferred_element_type=F32)
        merged = g_ref[rs, 0:d].astype(F32) * y_a + g_ref[rs, d:2 * d].astype(F32) * y_b
        return jnp.dot(merged.astype(BF16), wo_ref[...], preferred_element_type=F32)

    def ffn_chunk(h, c):
        w = min(ff_chunk, d_ff - c)
        gate = jnp.dot(h, wg_ref[:, c:c + w], preferred_element_type=F32)
        up = jnp.dot(h, wu_ref[:, c:c + w], preferred_element_type=F32)
        a = (jax.nn.silu(gate) * up).astype(BF16)
        return jnp.dot(a, wd_ref[c:c + w, :], preferred_element_type=F32)

    m = [mix(rs) for rs in groups]
    x1 = [x_ref[rs, :] + _rms(mg, lnm_ref[...]) for rs, mg in zip(groups, m)]
    h = [_rms(xg, lnp_ref[...]).astype(BF16) for xg in x1]
    f = [None] * len(groups)
    for c in range(0, d_ff, ff_chunk):
        for gi in range(len(groups)):
            part = ffn_chunk(h[gi], c)
            f[gi] = part if f[gi] is None else f[gi] + part
    for rs, xg, fg in zip(groups, x1, f):
        o_ref[rs, :] = xg + _rms(fg, lno_ref[...])


def _back(oa, ob, g, x2, wa, wb, wo, lnm, lnp, wg, wu, wd, lno, *, tm):
    t, d = x2.shape
    ff_chunk = 1024
    row = lambda w: pl.BlockSpec((tm, w), lambda i: (i, 0))
    weights = [wa, wb, wo, lnm, lnp, wg, wu, wd, lno]
    return pl.pallas_call(
        functools.partial(_back_kernel, d=d, ff_chunk=ff_chunk),
        grid=(t // tm,),
        in_specs=[row(d), row(d), row(2 * d), row(d)] + [_resident(w.shape) for w in weights],
        out_specs=row(d),
        out_shape=jax.ShapeDtypeStruct((t, d), F32),
        compiler_params=pltpu.CompilerParams(
            dimension_semantics=("parallel",), vmem_limit_bytes=VMEM_LIMIT),
        name="back",
    )(oa, ob, g, x2, *weights)


def _split_w_in(w, d):
    lat = MLA_Q_RANK + MLA_KV_RANK + ROPE_DIM
    zm_w = -(-lat // LANES) * LANES
    latent = jnp.pad(w[:, 3 * d:3 * d + lat].astype(BF16), ((0, 0), (0, zm_w - lat)))
    return [w[:, :3 * d].astype(BF16), latent, w[:, 3 * d + lat:].astype(BF16)]


def _pack_w_uq(w):
    r = w.shape[0]
    w = w.reshape(r, MLA_HEADS, MLA_QK_DIM)
    w = jnp.pad(w, ((0, 0), (0, 0), (0, MLA_HEAD_PAD - MLA_QK_DIM)))
    return w.reshape(r, MLA_HEADS * MLA_HEAD_PAD).astype(BF16)


def kernel(x, positions, ln_mix_pre, w_in, lambda_q1, lambda_k1, lambda_q2, lambda_k2, da_subln,
           q_a_norm, w_uq, kv_a_norm, w_ukv, w_proj_a, w_proj_b, w_o, ln_mix_post, ln_ffn_pre,
           w_ffn_gate, w_ffn_up, w_ffn_down, ln_ffn_post):
    bsz, seq, d = x.shape
    t = bsz * seq
    depth = w_in.shape[0]
    tm = 512

    half = ROPE_DIM // 2
    inv_freq = 1.0 / (ROPE_THETA ** (jnp.arange(half, dtype=F32) * (2.0 / ROPE_DIM)))
    inv_freq_lanes = jnp.tile(inv_freq, LANES // half).reshape(1, LANES)
    pos_lanes = jnp.broadcast_to(positions.reshape(t, 1), (t, LANES))

    x2 = x.reshape(t, d)
    for l in range(depth):
        lambda_init = 0.8 - 0.6 * math.exp(-0.3 * l)
        q_da, k_da, v_da, gates, q_b, k_b, v_b = _front(
            x2, pos_lanes, inv_freq_lanes, ln_mix_pre[l].reshape(1, d), _split_w_in(w_in[l], d),
            q_a_norm[l].reshape(1, -1), kv_a_norm[l].reshape(1, -1),
            _pack_w_uq(w_uq[l]), w_ukv[l].astype(BF16),
            bsz=bsz, seq=seq, d=d, tm=tm,
            qa_scale=DA_HEAD_DIM ** -0.5 * LOG2E, qb_scale=MLA_QK_DIM ** -0.5 * LOG2E)

        lam_p = jnp.stack([lambda_q1[l], lambda_k1[l], lambda_q2[l], lambda_k2[l]]).astype(F32)
        lam_p = jnp.pad(lam_p, ((0, 0), (0, LANES - DA_HEAD_DIM)))
        oa = _flash(q_da, k_da, v_da, [lam_p, da_subln[l].reshape(1, -1)], blk=1024, dual=True,
                    lambda_init=lambda_init)
        ob = _flash(q_b, k_b, v_b, [], blk=1024, dual=False)

        x2 = _back(oa.reshape(t, d), ob.reshape(t, d), gates, x2,
                   w_proj_a[l].astype(BF16), w_proj_b[l].astype(BF16), w_o[l].astype(BF16),
                   ln_mix_post[l].reshape(1, d), ln_ffn_pre[l].reshape(1, d),
                   w_ffn_gate[l].astype(BF16), w_ffn_up[l].astype(BF16),
                   w_ffn_down[l].astype(BF16), ln_ffn_post[l].reshape(1, d), tm=tm)
    return x2.reshape(bsz, seq, d)
```
